```python
import jax, jax.numpy as jnp
from jax import lax
import numpy as np

D_MODEL = 1024
BATCH = 2
SEQ = 8192
DEPTH = 2

CONV_CH = D_MODEL // 2
CONV_K = 3
SGU_WIDTH = D_MODEL // 2
SGU_GROUPS = 4
SGU_GROUP_CH = SGU_WIDTH // SGU_GROUPS
CHUNK = 128
ATT_HEADS = 8
HEAD_DIM = 64
ATT_WIDTH = ATT_HEADS * HEAD_DIM
Q_BLOCK = 128
N_BRANCH = 3
D_FF = ((8 * D_MODEL + 3 * 256 - 1) // (3 * 256)) * 256
IN_COLS = 3 * CONV_CH + 2 * SGU_WIDTH + 3 * ATT_WIDTH + N_BRANCH * D_MODEL
EPS = 1e-6

kernel_name = "hybrid_gated_conv_sgu_stickbreaking"


def _rmsnorm(x, g):
    xf = x.astype(jnp.float32)
    y = xf * lax.rsqrt(jnp.mean(xf * xf, axis=-1, keepdims=True) + EPS)
    return (y * g.astype(jnp.float32)).astype(x.dtype)


def _layernorm(x, g, b):
    xf = x.astype(jnp.float32)
    mu = jnp.mean(xf, axis=-1, keepdims=True)
    xc = xf - mu
    y = xc * lax.rsqrt(jnp.mean(xc * xc, axis=-1, keepdims=True) + EPS)
    return (y * g.astype(jnp.float32) + b.astype(jnp.float32)).astype(x.dtype)


def _split_cols(p):
    widths = (CONV_CH, CONV_CH, CONV_CH, SGU_WIDTH, SGU_WIDTH,
              ATT_WIDTH, ATT_WIDTH, ATT_WIDTH, N_BRANCH * D_MODEL)
    out, off = [], 0
    for w in widths:
        out.append(p[..., off:off + w])
        off += w
    return out


def _short_conv(bg, cg, xa, w, b):
    u = cg * xa
    S = u.shape[1]
    up = jnp.pad(u, ((0, 0), (CONV_K - 1, 0), (0, 0)))
    y = b
    for k in range(CONV_K):
        y = y + w[k] * up[:, k:k + S]
    return bg * y


def _sgu(u, v, ln_g, ln_b, w_s, b_s):
    u = jax.nn.gelu(u, approximate=False)
    v = _layernorm(jax.nn.gelu(v, approximate=False), ln_g, ln_b)
    B, S, _ = v.shape
    vc = v.reshape(B, S // CHUNK, CHUNK, SGU_GROUPS, SGU_GROUP_CH)
    causal = jnp.tril(jnp.ones((CHUNK, CHUNK), dtype=bool))
    w = jnp.where(causal[None], w_s, 0.0).astype(v.dtype)
    mixed = jnp.einsum('gts,bnsgc->bntgc', w, vc) + b_s.T[None, None, :, :, None]
    return u * mixed.reshape(B, S, SGU_WIDTH)


def _stick_breaking(q, k, v):
    B, H, S, d = q.shape
    nb = S // Q_BLOCK
    scale = 1.0 / float(np.sqrt(d))
    qb = q.reshape(B, H, nb, Q_BLOCK, d).transpose(2, 0, 1, 3, 4)
    key_pos = jnp.arange(S)

    def block(args):
        qi, i = args
        z = jnp.einsum('bhqd,bhkd->bhqk', qi, k).astype(jnp.float32) * scale
        qpos = i * Q_BLOCK + jnp.arange(Q_BLOCK)
        mask = key_pos[None, :] < qpos[:, None]
        log_beta = jax.nn.log_sigmoid(z)
        log_1m = jnp.where(mask, jax.nn.log_sigmoid(-z), 0.0)
        after = lax.cumsum(log_1m, axis=3, reverse=True) - log_1m
        a = jnp.where(mask, jnp.exp(log_beta + after), 0.0)
        return jnp.einsum('bhqk,bhkd->bhqd', a.astype(v.dtype), v)

    out = lax.map(block, (qb, jnp.arange(nb)))
    return out.transpose(1, 2, 0, 3, 4).reshape(B, H, S, d)


def setup_inputs(seed: int = 0) -> dict:
    key = jax.random.key(seed)
    ks = jax.random.split(key, 17)
    L, D = DEPTH, D_MODEL
    nrm = lambda k, shape, s: jax.random.normal(k, shape, jnp.float32) * s
    return {
        "x": jax.random.normal(ks[0], (BATCH, SEQ, D), jnp.float32),
        "mix_norm_g": 1.0 + nrm(ks[1], (L, D), 0.02),
        "w_in": nrm(ks[2], (L, D, IN_COLS), D ** -0.5),
        "b_gate": nrm(ks[3], (L, N_BRANCH * D), 0.02),
        "conv_w": nrm(ks[4], (L, CONV_K, CONV_CH), CONV_K ** -0.5),
        "conv_b": nrm(ks[5], (L, CONV_CH), 0.02),
        "sgu_ln_g": 1.0 + nrm(ks[6], (L, SGU_WIDTH), 0.02),
        "sgu_ln_b": nrm(ks[7], (L, SGU_WIDTH), 0.02),
        "sgu_w": nrm(ks[8], (L, SGU_GROUPS, CHUNK, CHUNK), CHUNK ** -0.5),
        "sgu_b": nrm(ks[9], (L, SGU_GROUPS, CHUNK), 0.02),
        "q_norm_g": 1.0 + nrm(ks[10], (L, HEAD_DIM), 0.02),
        "k_norm_g": 1.0 + nrm(ks[11], (L, HEAD_DIM), 0.02),
        "w_branch_out": nrm(ks[12], (L, N_BRANCH, CONV_CH, D), CONV_CH ** -0.5),
        "w_o": nrm(ks[13], (L, D, D), D ** -0.5),
        "ffn_norm_g": 1.0 + nrm(ks[14], (L, D), 0.02),
        "w_gate_up": nrm(ks[15], (L, D, 2 * D_FF), D ** -0.5),
        "w_down": nrm(ks[16], (L, D_FF, D), D_FF ** -0.5),
    }


def reference(x, mix_norm_g, w_in, b_gate, conv_w, conv_b, sgu_ln_g, sgu_ln_b, sgu_w, sgu_b,
              q_norm_g, k_norm_g, w_branch_out, w_o, ffn_norm_g, w_gate_up, w_down):
    B, S, D = x.shape
    for l in range(DEPTH):
        h = _rmsnorm(x, mix_norm_g[l])
        a_b, a_c, a_x, s_u, s_v, q, k, v, gates = _split_cols(h @ w_in[l])

        ya = _short_conv(a_b, a_c, a_x, conv_w[l], conv_b[l])
        yb = _sgu(s_u, s_v, sgu_ln_g[l], sgu_ln_b[l], sgu_w[l], sgu_b[l])

        qh = _rmsnorm(q.reshape(B, S, ATT_HEADS, HEAD_DIM), q_norm_g[l]).transpose(0, 2, 1, 3)
        kh = _rmsnorm(k.reshape(B, S, ATT_HEADS, HEAD_DIM), k_norm_g[l]).transpose(0, 2, 1, 3)
        vh = v.reshape(B, S, ATT_HEADS, HEAD_DIM).transpose(0, 2, 1, 3)
        yc = _stick_breaking(qh, kh, vh).transpose(0, 2, 1, 3).reshape(B, S, ATT_WIDTH)

        ys = jnp.stack([ya, yb, yc], axis=2)
        yd = jnp.einsum('bsnc,ncd->bsnd', ys, w_branch_out[l])
        g = jax.nn.sigmoid(gates + b_gate[l]).reshape(B, S, N_BRANCH, D)
        merged = jnp.sum(g * yd, axis=2)
        x = x + merged @ w_o[l]

        h2 = _rmsnorm(x, ffn_norm_g[l])
        gu = h2 @ w_gate_up[l]
        x = x + (jax.nn.silu(gu[..., :D_FF]) * gu[..., D_FF:]) @ w_down[l]
    return x
```

```python
import functools

import jax
import jax.numpy as jnp
from jax import lax
from jax.experimental import pallas as pl
from jax.experimental.pallas import tpu as pltpu

F32 = jnp.float32
BF16 = jnp.bfloat16

D_MODEL = 1024
CONV_CH = 512
CONV_K = 3
SGU_WIDTH = 512
SGU_GROUPS = 4
CHUNK = 128
ATT_HEADS = 8
HEAD_DIM = 64
ATT_WIDTH = ATT_HEADS * HEAD_DIM
N_BRANCH = 3
D_FF = 2816
IN_COLS = 7168
EPS = 1e-6

LANES = 128
SUBLANES = 8
VMEM_LIMIT_BYTES = 56 * 1024 * 1024

C_AB, C_AC, C_AX = 0, 512, 1024
C_SU, C_SV = 1536, 2048
C_Q, C_K, C_V = 2560, 3072, 3584
C_G = 4096

TM_IN = 512
TM_FFN = 512
TQ_OUTER = 1024
TQ = 256
TK = 256
FF_SPLIT = 1408

EXP_ZERO_BOUND = -104.0


def _sigmoid(x):
    return 1.0 / (1.0 + jnp.exp(-x))


def _gelu_exact(x):
    return 0.5 * x * (1.0 + lax.erf(x * (2.0 ** -0.5)))


def _softplus(x):
    return jnp.maximum(x, 0.0) + jnp.log1p(jnp.exp(-jnp.abs(x)))


def _dot(a, b):
    return jnp.dot(a, b, preferred_element_type=F32)


def _resident(shape):
    zeros = (0,) * len(shape)
    return pl.BlockSpec(shape, lambda *_: zeros, pipeline_mode=pl.Buffered(1))


def _head_rmsnorm(p, g128):
    tm = p.shape[0]
    lo = lax.broadcasted_iota(jnp.int32, (tm, LANES), 1) < HEAD_DIM
    outs = []
    for c in range(ATT_WIDTH // LANES):
        pc = p[:, c * LANES:(c + 1) * LANES]
        pp = pc * pc
        s_lo = jnp.sum(jnp.where(lo, pp, 0.0), axis=-1, keepdims=True)
        s_hi = jnp.sum(jnp.where(lo, 0.0, pp), axis=-1, keepdims=True)
        ms = jnp.where(lo, s_lo, s_hi) * (1.0 / HEAD_DIM)
        outs.append(pc * lax.rsqrt(ms + EPS) * g128)
    return jnp.concatenate(outs, axis=-1)


def _mixer_in_body(x_ref, ng_ref, win_ref, bg_ref, cw_ref, cb_ref, lng_ref, lnb_ref,
                   sw_ref, sb_ref, qg_ref, kg_ref, wab_ref,
                   mab_ref, gc_ref, q_ref, k_ref, v_ref,
                   u_scr, yb_scr):
    tm = x_ref.shape[1]
    j = pl.program_id(1)

    x = x_ref[0]
    ms = jnp.mean(x * x, axis=-1, keepdims=True)
    h = (x * lax.rsqrt(ms + EPS) * ng_ref[...]).astype(BF16)

    def proj(c0, width):
        return _dot(h, win_ref[:, c0:c0 + width])

    @pl.when(j == 0)
    def _():
        u_scr[0:SUBLANES, :] = jnp.zeros((SUBLANES, CONV_CH), F32)

    u = proj(C_AC, CONV_CH) * proj(C_AX, CONV_CH)
    u_scr[SUBLANES:SUBLANES + tm, :] = u
    u1 = u_scr[SUBLANES - 1:SUBLANES - 1 + tm, :]
    u2 = u_scr[SUBLANES - 2:SUBLANES - 2 + tm, :]
    y = cb_ref[...] + cw_ref[0:1, :] * u2
    y = y + cw_ref[1:2, :] * u1
    y = y + cw_ref[2:3, :] * u
    ya = (proj(C_AB, CONV_CH) * y).astype(BF16)
    u_scr[0:SUBLANES, :] = u_scr[tm:tm + SUBLANES, :]

    gu = _gelu_exact(proj(C_SU, SGU_WIDTH))
    gv = _gelu_exact(proj(C_SV, SGU_WIDTH))
    mu = jnp.mean(gv, axis=-1, keepdims=True)
    vc = gv - mu
    var = jnp.mean(vc * vc, axis=-1, keepdims=True)
    vn = (vc * lax.rsqrt(var + EPS) * lng_ref[...] + lnb_ref[...]).astype(BF16)
    row = lax.broadcasted_iota(jnp.int32, (CHUNK, CHUNK), 0)
    col = lax.broadcasted_iota(jnp.int32, (CHUNK, CHUNK), 1)
    tril = col <= row
    for g in range(SGU_GROUPS):
        wg = jnp.where(tril, sw_ref[g], 0.0).astype(BF16)
        cs = slice(g * CHUNK, (g + 1) * CHUNK)
        for c in range(tm // CHUNK):
            rs = slice(c * CHUNK, (c + 1) * CHUNK)
            mixed = _dot(wg, vn[rs, cs]) + sb_ref[:, cs]
            yb_scr[rs, cs] = (gu[rs, cs] * mixed).astype(BF16)

    q = _head_rmsnorm(proj(C_Q, ATT_WIDTH), qg_ref[...]) * (HEAD_DIM ** -0.5)
    q_ref[0] = q.astype(BF16)
    k_ref[0] = _head_rmsnorm(proj(C_K, ATT_WIDTH), kg_ref[...]).astype(BF16)
    v_ref[0] = proj(C_V, ATT_WIDTH).astype(BF16)

    ga = _sigmoid(proj(C_G, D_MODEL) + bg_ref[:, 0:D_MODEL])
    mab = ga * _dot(ya, wab_ref[0])
    gb = _sigmoid(proj(C_G + D_MODEL, D_MODEL) + bg_ref[:, D_MODEL:2 * D_MODEL])
    mab = mab + gb * _dot(yb_scr[...], wab_ref[1])
    mab_ref[0] = mab.astype(BF16)
    gc = _sigmoid(proj(C_G + 2 * D_MODEL, D_MODEL) + bg_ref[:, 2 * D_MODEL:3 * D_MODEL])
    gc_ref[0] = gc.astype(BF16)


def _mixer_in(x, ng, win, bg, cw, cb, lng, lnb, sw, sbf, qg, kg, wab):
    B, S, D = x.shape
    tm = TM_IN
    tok = lambda width: pl.BlockSpec((1, tm, width), lambda b, j: (b, j, 0))
    out_shape = (
        jax.ShapeDtypeStruct((B, S, D), BF16),
        jax.ShapeDtypeStruct((B, S, D), BF16),
        jax.ShapeDtypeStruct((B, S, ATT_WIDTH), BF16),
        jax.ShapeDtypeStruct((B, S, ATT_WIDTH), BF16),
        jax.ShapeDtypeStruct((B, S, ATT_WIDTH), BF16),
    )
    return pl.pallas_call(
        _mixer_in_body,
        grid=(B, S // tm),
        in_specs=[
            tok(D),
            _resident(ng.shape), _resident(win.shape), _resident(bg.shape),
            _resident(cw.shape), _resident(cb.shape), _resident(lng.shape),
            _resident(lnb.shape), _resident(sw.shape), _resident(sbf.shape),
            _resident(qg.shape), _resident(kg.shape), _resident(wab.shape),
        ],
        out_specs=(tok(D), tok(D), tok(ATT_WIDTH), tok(ATT_WIDTH), tok(ATT_WIDTH)),
        out_shape=out_shape,
        scratch_shapes=[
            pltpu.VMEM((tm + SUBLANES, CONV_CH), F32),
            pltpu.VMEM((tm, SGU_WIDTH), BF16),
        ],
        compiler_params=pltpu.CompilerParams(
            dimension_semantics=("arbitrary", "arbitrary"),
            vmem_limit_bytes=VMEM_LIMIT_BYTES),
        name="mixer_in",
    )(x, ng, win, bg, cw, cb, lng, lnb, sw, sbf, qg, kg, wab)


def _attention_body(q_ref, k_ref, v_ref, o_ref, acc_scr, carry_scr):
    io = pl.program_id(2)
    lane_q = lax.broadcasted_iota(jnp.int32, (TQ, LANES), 1)
    lo = lane_q < HEAD_DIM
    upper = (lax.broadcasted_iota(jnp.int32, (TK, TK), 0)
             > lax.broadcasted_iota(jnp.int32, (TK, TK), 1)).astype(BF16)
    row = lax.broadcasted_iota(jnp.int32, (TQ, TK), 0)
    col = lax.broadcasted_iota(jnp.int32, (TQ, TK), 1)

    def q_block(qi, _):
        q_start = pl.multiple_of(io * TQ_OUTER + qi * TQ, TQ)
        q = q_ref[0, pl.ds(pl.multiple_of(qi * TQ, TQ), TQ), :]
        zero = jnp.zeros_like(q)
        q_heads = (jnp.where(lo, q, zero), jnp.where(lo, zero, q))
        acc_scr[...] = jnp.zeros(acc_scr.shape, F32)
        carry_scr[...] = jnp.zeros(carry_scr.shape, F32)

        def cond(state):
            kb, go = state
            return jnp.logical_and(kb >= 0, go)

        def body(state):
            kb, _ = state
            k_start = pl.multiple_of(kb * TK, TK)
            kblk = k_ref[0, pl.ds(k_start, TK), :]
            vblk = v_ref[0, pl.ds(k_start, TK), :]
            mask = (col + k_start) < (row + q_start)
            worst = None
            for hd in range(2):
                z = lax.dot_general(q_heads[hd], kblk, (((1,), (1,)), ((), ())),
                                    preferred_element_type=F32)
                sp = _softplus(z)
                log_1m = jnp.where(mask, -sp, 0.0)
                l_hi = log_1m.astype(BF16)
                l_lo = (log_1m - l_hi.astype(F32)).astype(BF16)
                after = _dot(l_hi, upper) + _dot(l_lo, upper)
                carry = carry_scr[hd]
                carry_k = jnp.concatenate([carry] * (TK // LANES), axis=-1)
                a = jnp.where(mask, jnp.exp(z - sp + after + carry_k), 0.0)
                acc_scr[hd] += _dot(a.astype(BF16), vblk)
                new_carry = carry + jnp.sum(log_1m, axis=-1, keepdims=True)
                carry_scr[hd] = new_carry
                m = jnp.max(new_carry)
                worst = m if worst is None else jnp.maximum(worst, m)
            return kb - 1, worst > EXP_ZERO_BOUND

        diag_block = io * (TQ_OUTER // TK) + qi * (TQ // TK)
        lax.while_loop(cond, body, (diag_block, jnp.bool_(True)))
        out = jnp.where(lo, acc_scr[0], acc_scr[1])
        o_ref[0, pl.ds(pl.multiple_of(qi * TQ, TQ), TQ), :] = out.astype(o_ref.dtype)
        return 0

    lax.fori_loop(0, TQ_OUTER // TQ, q_block, 0)


def _attention(q, k, v):
    B, S, W = q.shape
    n_pairs = W // LANES
    qspec = pl.BlockSpec((1, TQ_OUTER, LANES), lambda b, p, i: (b, i, p))
    kvspec = pl.BlockSpec((1, S, LANES), lambda b, p, i: (b, 0, p))
    return pl.pallas_call(
        _attention_body,
        grid=(B, n_pairs, S // TQ_OUTER),
        in_specs=[qspec, kvspec, kvspec],
        out_specs=qspec,
        out_shape=jax.ShapeDtypeStruct((B, S, W), BF16),
        scratch_shapes=[
            pltpu.VMEM((2, TQ, LANES), F32),
            pltpu.VMEM((2, TQ, LANES), F32),
        ],
        compiler_params=pltpu.CompilerParams(
            dimension_semantics=("arbitrary", "arbitrary", "arbitrary"),
            vmem_limit_bytes=VMEM_LIMIT_BYTES),
        name="attention",
    )(q, k, v)


def _merge_ffn_body(x_ref, mab_ref, gc_ref, yc_ref, wc_ref, wo_ref, fg_ref, wgu_ref, wd_ref,
                    o_ref, act_scr):
    ydc = _dot(yc_ref[...], wc_ref[...])
    merged = mab_ref[...].astype(F32) + gc_ref[...].astype(F32) * ydc
    x1 = x_ref[...] + _dot(merged.astype(BF16), wo_ref[...])
    o_ref[...] = x1
    ms = jnp.mean(x1 * x1, axis=-1, keepdims=True)
    h2 = (x1 * lax.rsqrt(ms + EPS) * fg_ref[...]).astype(BF16)
    for c0 in range(0, D_FF, FF_SPLIT):
        gate = _dot(h2, wgu_ref[:, c0:c0 + FF_SPLIT])
        up = _dot(h2, wgu_ref[:, D_FF + c0:D_FF + c0 + FF_SPLIT])
        act_scr[:, c0:c0 + FF_SPLIT] = (gate * _sigmoid(gate) * up).astype(BF16)
    o_ref[...] += _dot(act_scr[...], wd_ref[...])


def _merge_ffn(x, mab, gc, yc, wc, wo, fg, wgu, wd):
    M, D = x.shape
    tm = TM_FFN
    tok = lambda width: pl.BlockSpec((tm, width), lambda i: (i, 0))
    return pl.pallas_call(
        _merge_ffn_body,
        grid=(M // tm,),
        in_specs=[
            tok(D), tok(D), tok(D), tok(ATT_WIDTH),
            _resident(wc.shape), _resident(wo.shape), _resident(fg.shape),
            _resident(wgu.shape), _resident(wd.shape),
        ],
        out_specs=tok(D),
        out_shape=jax.ShapeDtypeStruct((M, D), F32),
        scratch_shapes=[pltpu.VMEM((tm, D_FF), BF16)],
        compiler_params=pltpu.CompilerParams(
            dimension_semantics=("arbitrary",),
            vmem_limit_bytes=VMEM_LIMIT_BYTES),
        name="merge_ffn",
    )(x, mab, gc, yc, wc, wo, fg, wgu, wd)


def kernel(x, mix_norm_g, w_in, b_gate, conv_w, conv_b, sgu_ln_g, sgu_ln_b, sgu_w, sgu_b,
           q_norm_g, k_norm_g, w_branch_out, w_o, ffn_norm_g, w_gate_up, w_down):
    B, S, D = x.shape
    depth = w_in.shape[0]
    row = lambda a: a.reshape(1, -1)
    for l in range(depth):
        sbf = jnp.repeat(sgu_b[l].T, CHUNK, axis=1)
        qg = jnp.tile(q_norm_g[l], LANES // HEAD_DIM).reshape(1, LANES)
        kg = jnp.tile(k_norm_g[l], LANES // HEAD_DIM).reshape(1, LANES)
        mab, gc, q, k, v = _mixer_in(
            x, row(mix_norm_g[l]), w_in[l].astype(BF16), row(b_gate[l]),
            conv_w[l], row(conv_b[l]), row(sgu_ln_g[l]), row(sgu_ln_b[l]),
            sgu_w[l], sbf, qg, kg, w_branch_out[l, 0:2].astype(BF16))
        yc = _attention(q, k, v)
        x = _merge_ffn(
            x.reshape(B * S, D), mab.reshape(B * S, D), gc.reshape(B * S, D),
            yc.reshape(B * S, ATT_WIDTH), w_branch_out[l, 2].astype(BF16),
            w_o[l].astype(BF16), row(ffn_norm_g[l]), w_gate_up[l].astype(BF16),
            w_down[l].astype(BF16)).reshape(B, S, D)
    return x
```

```python
import functools

import jax
import jax.numpy as jnp
from jax import lax
from jax.experimental import pallas as pl
from jax.experimental.pallas import tpu as pltpu

F32 = jnp.float32
BF16 = jnp.bfloat16

D_MODEL = 1024
CONV_CH = 512
CONV_K = 3
SGU_WIDTH = 512
SGU_GROUPS = 4
CHUNK = 128
ATT_HEADS = 8
HEAD_DIM = 64
ATT_WIDTH = ATT_HEADS * HEAD_DIM
N_BRANCH = 3
D_FF = 2816
IN_COLS = 7168
EPS = 1e-6

LANES = 128
SUBLANES = 8
VMEM_LIMIT_BYTES = 56 * 1024 * 1024

C_AB, C_AC, C_AX = 0, 512, 1024
C_SU, C_SV = 1536, 2048
C_Q, C_K, C_V = 2560, 3072, 3584
C_G = 4096

TM_IN = 512
TM_FFN = 512
TQ_OUTER = 1024
TQ = 256
TK = 256
FF_SPLIT = 1408

EXP_ZERO_BOUND = -104.0


def _sigmoid(x):
    return 1.0 / (1.0 + jnp.exp(-x))


def _gelu_exact(x):
    return 0.5 * x * (1.0 + lax.erf(x * (2.0 ** -0.5)))


def _softplus(x):
    return jnp.maximum(x, 0.0) + jnp.log(1.0 + jnp.exp(-jnp.abs(x)))


def _dot(a, b):
    return jnp.dot(a, b, preferred_element_type=F32)


def _resident(shape):
    zeros = (0,) * len(shape)
    return pl.BlockSpec(shape, lambda *_: zeros, pipeline_mode=pl.Buffered(1))


def _head_rmsnorm(p, g128):
    tm = p.shape[0]
    lo = lax.broadcasted_iota(jnp.int32, (tm, LANES), 1) < HEAD_DIM
    outs = []
    for c in range(ATT_WIDTH // LANES):
        pc = p[:, c * LANES:(c + 1) * LANES]
        pp = pc * pc
        s_lo = jnp.sum(jnp.where(lo, pp, 0.0), axis=-1, keepdims=True)
        s_hi = jnp.sum(jnp.where(lo, 0.0, pp), axis=-1, keepdims=True)
        ms = jnp.where(lo, s_lo, s_hi) * (1.0 / HEAD_DIM)
        outs.append(pc * lax.rsqrt(ms + EPS) * g128)
    return jnp.concatenate(outs, axis=-1)


def _mixer_in_body(x_ref, ng_ref, win_ref, bg_ref, cw_ref, cb_ref, lng_ref, lnb_ref,
                   sw_ref, sb_ref, qg_ref, kg_ref, wab_ref,
                   mab_ref, gc_ref, q_ref, k_ref, v_ref,
                   u_scr, yb_scr):
    tm = x_ref.shape[1]
    j = pl.program_id(1)

    x = x_ref[0]
    ms = jnp.mean(x * x, axis=-1, keepdims=True)
    h = (x * lax.rsqrt(ms + EPS) * ng_ref[...]).astype(BF16)

    def proj(c0, width):
        return _dot(h, win_ref[:, c0:c0 + width])

    @pl.when(j == 0)
    def _():
        u_scr[0:SUBLANES, :] = jnp.zeros((SUBLANES, CONV_CH), F32)

    u = proj(C_AC, CONV_CH) * proj(C_AX, CONV_CH)
    u_scr[SUBLANES:SUBLANES + tm, :] = u
    u1 = u_scr[SUBLANES - 1:SUBLANES - 1 + tm, :]
    u2 = u_scr[SUBLANES - 2:SUBLANES - 2 + tm, :]
    y = cb_ref[...] + cw_ref[0:1, :] * u2
    y = y + cw_ref[1:2, :] * u1
    y = y + cw_ref[2:3, :] * u
    ya = (proj(C_AB, CONV_CH) * y).astype(BF16)
    u_scr[0:SUBLANES, :] = u_scr[tm:tm + SUBLANES, :]

    gu = _gelu_exact(proj(C_SU, SGU_WIDTH))
    gv = _gelu_exact(proj(C_SV, SGU_WIDTH))
    mu = jnp.mean(gv, axis=-1, keepdims=True)
    vc = gv - mu
    var = jnp.mean(vc * vc, axis=-1, keepdims=True)
    vn = (vc * lax.rsqrt(var + EPS) * lng_ref[...] + lnb_ref[...]).astype(BF16)
    row = lax.broadcasted_iota(jnp.int32, (CHUNK, CHUNK), 0)
    col = lax.broadcasted_iota(jnp.int32, (CHUNK, CHUNK), 1)
    tril = col <= row
    for g in range(SGU_GROUPS):
        wg = jnp.where(tril, sw_ref[g], 0.0).astype(BF16)
        cs = slice(g * CHUNK, (g + 1) * CHUNK)
        for c in range(tm // CHUNK):
            rs = slice(c * CHUNK, (c + 1) * CHUNK)
            mixed = _dot(wg, vn[rs, cs]) + sb_ref[:, cs]
            yb_scr[rs, cs] = (gu[rs, cs] * mixed).astype(BF16)

    q = _head_rmsnorm(proj(C_Q, ATT_WIDTH), qg_ref[...]) * (HEAD_DIM ** -0.5)
    q_ref[0] = q.astype(BF16)
    k_ref[0] = _head_rmsnorm(proj(C_K, ATT_WIDTH), kg_ref[...]).astype(BF16)
    v_ref[0] = proj(C_V, ATT_WIDTH).astype(BF16)

    ga = _sigmoid(proj(C_G, D_MODEL) + bg_ref[:, 0:D_MODEL])
    mab = ga * _dot(ya, wab_ref[0])
    gb = _sigmoid(proj(C_G + D_MODEL, D_MODEL) + bg_ref[:, D_MODEL:2 * D_MODEL])
    mab = mab + gb * _dot(yb_scr[...], wab_ref[1])
    mab_ref[0] = mab.astype(BF16)
    gc = _sigmoid(proj(C_G + 2 * D_MODEL, D_MODEL) + bg_ref[:, 2 * D_MODEL:3 * D_MODEL])
    gc_ref[0] = gc.astype(BF16)


def _mixer_in(x, ng, win, bg, cw, cb, lng, lnb, sw, sbf, qg, kg, wab):
    B, S, D = x.shape
    tm = TM_IN
    tok = lambda width: pl.BlockSpec((1, tm, width), lambda b, j: (b, j, 0))
    out_shape = (
        jax.ShapeDtypeStruct((B, S, D), BF16),
        jax.ShapeDtypeStruct((B, S, D), BF16),
        jax.ShapeDtypeStruct((B, S, ATT_WIDTH), BF16),
        jax.ShapeDtypeStruct((B, S, ATT_WIDTH), BF16),
        jax.ShapeDtypeStruct((B, S, ATT_WIDTH), BF16),
    )
    return pl.pallas_call(
        _mixer_in_body,
        grid=(B, S // tm),
        in_specs=[
            tok(D),
            _resident(ng.shape), _resident(win.shape), _resident(bg.shape),
            _resident(cw.shape), _resident(cb.shape), _resident(lng.shape),
            _resident(lnb.shape), _resident(sw.shape), _resident(sbf.shape),
            _resident(qg.shape), _resident(kg.shape), _resident(wab.shape),
        ],
        out_specs=(tok(D), tok(D), tok(ATT_WIDTH), tok(ATT_WIDTH), tok(ATT_WIDTH)),
        out_shape=out_shape,
        scratch_shapes=[
            pltpu.VMEM((tm + SUBLANES, CONV_CH), F32),
            pltpu.VMEM((tm, SGU_WIDTH), BF16),
        ],
        compiler_params=pltpu.CompilerParams(
            dimension_semantics=("arbitrary", "arbitrary"),
            vmem_limit_bytes=VMEM_LIMIT_BYTES),
        name="mixer_in",
    )(x, ng, win, bg, cw, cb, lng, lnb, sw, sbf, qg, kg, wab)


def _attention_body(q_ref, k_ref, v_ref, o_ref, acc_scr, carry_scr):
    io = pl.program_id(2)
    n_q = TQ_OUTER // TQ
    lo = lax.broadcasted_iota(jnp.int32, (TQ, LANES), 1) < HEAD_DIM
    neg_incl = jnp.where(lax.broadcasted_iota(jnp.int32, (TK, TK), 0)
                         >= lax.broadcasted_iota(jnp.int32, (TK, TK), 1), -1.0, 0.0).astype(BF16)
    neg_incl2 = jnp.concatenate([neg_incl, neg_incl], axis=0)
    diag_mask = (lax.broadcasted_iota(jnp.int32, (TQ, TK), 1)
                 < lax.broadcasted_iota(jnp.int32, (TQ, TK), 0))

    def split_heads(q):
        zero = jnp.zeros_like(q)
        return jnp.where(lo, q, zero), jnp.where(lo, zero, q)

    def key_block(qh, kblk, vblk, carry, mask):
        z = lax.dot_general(qh, kblk, (((1,), (1,)), ((), ())), preferred_element_type=F32)
        sp = _softplus(z)
        if mask is not None:
            sp = jnp.where(mask, sp, 0.0)
        sp_hi = sp.astype(BF16)
        sp_lo = (sp - sp_hi.astype(F32)).astype(BF16)
        e = z + _dot(jnp.concatenate([sp_hi, sp_lo], axis=1), neg_incl2)
        if carry is not None:
            e = e + carry
        a = jnp.exp(e)
        if mask is not None:
            a = jnp.where(mask, a, 0.0)
        return _dot(a.astype(BF16), vblk), jnp.sum(sp, axis=-1, keepdims=True)

    for qi in range(n_q):
        q_start = pl.multiple_of(io * TQ_OUTER + qi * TQ, TQ)
        prev_start = pl.multiple_of(jnp.maximum(q_start - TK, 0), TK)
        q_heads = split_heads(q_ref[0, qi * TQ:(qi + 1) * TQ, :])
        k_diag = k_ref[0, pl.ds(q_start, TK), :]
        v_diag = v_ref[0, pl.ds(q_start, TK), :]
        k_prev = k_ref[0, pl.ds(prev_start, TK), :]
        v_prev = v_ref[0, pl.ds(prev_start, TK), :]
        prev_mask = None if qi > 0 else jnp.broadcast_to(io > 0, (TQ, TK))
        for hd in range(2):
            pv_d, rs_d = key_block(q_heads[hd], k_diag, v_diag, None, diag_mask)
            pv_p, rs_p = key_block(q_heads[hd], k_prev, v_prev, -rs_d, prev_mask)
            acc_scr[qi, hd] = pv_d + pv_p
            carry_scr[qi, hd] = jnp.broadcast_to(-(rs_d + rs_p), (TQ, LANES))

    def finish(qi, _):
        q_heads = split_heads(q_ref[0, pl.ds(pl.multiple_of(qi * TQ, TQ), TQ), :])

        def worst_carry():
            return jnp.maximum(jnp.max(carry_scr[qi, 0]), jnp.max(carry_scr[qi, 1]))

        def cond(state):
            kb, go = state
            return jnp.logical_and(kb >= 0, go)

        def body(state):
            kb, _ = state
            k_start = pl.multiple_of(kb * TK, TK)
            kblk = k_ref[0, pl.ds(k_start, TK), :]
            vblk = v_ref[0, pl.ds(k_start, TK), :]
            for hd in range(2):
                carry = carry_scr[qi, hd]
                pv, rs = key_block(q_heads[hd], kblk, vblk, carry[:, 0:1], None)
                acc_scr[qi, hd] += pv
                carry_scr[qi, hd] = carry - rs
            return kb - 1, worst_carry() > EXP_ZERO_BOUND

        first_far_block = io * (TQ_OUTER // TK) + qi - 2
        lax.while_loop(cond, body, (first_far_block, worst_carry() > EXP_ZERO_BOUND))
        out = jnp.where(lo, acc_scr[qi, 0], acc_scr[qi, 1])
        o_ref[0, pl.ds(pl.multiple_of(qi * TQ, TQ), TQ), :] = out.astype(o_ref.dtype)
        return 0

    lax.fori_loop(0, n_q, finish, 0)


def _attention(q, k, v):
    B, S, W = q.shape
    n_pairs = W // LANES
    qspec = pl.BlockSpec((1, TQ_OUTER, LANES), lambda b, p, i: (b, i, p))
    kvspec = pl.BlockSpec((1, S, LANES), lambda b, p, i: (b, 0, p))
    return pl.pallas_call(
        _attention_body,
        grid=(B, n_pairs, S // TQ_OUTER),
        in_specs=[qspec, kvspec, kvspec],
        out_specs=qspec,
        out_shape=jax.ShapeDtypeStruct((B, S, W), BF16),
        scratch_shapes=[
            pltpu.VMEM((TQ_OUTER // TQ, 2, TQ, LANES), F32),
            pltpu.VMEM((TQ_OUTER // TQ, 2, TQ, LANES), F32),
        ],
        compiler_params=pltpu.CompilerParams(
            dimension_semantics=("arbitrary", "arbitrary", "arbitrary"),
            vmem_limit_bytes=VMEM_LIMIT_BYTES),
        name="attention",
    )(q, k, v)


def _merge_ffn_body(x_ref, mab_ref, gc_ref, yc_ref, wc_ref, wo_ref, fg_ref, wgu_ref, wd_ref,
                    o_ref, act_scr):
    ydc = _dot(yc_ref[...], wc_ref[...])
    merged = mab_ref[...].astype(F32) + gc_ref[...].astype(F32) * ydc
    x1 = x_ref[...] + _dot(merged.astype(BF16), wo_ref[...])
    o_ref[...] = x1
    ms = jnp.mean(x1 * x1, axis=-1, keepdims=True)
    h2 = (x1 * lax.rsqrt(ms + EPS) * fg_ref[...]).astype(BF16)
    for c0 in range(0, D_FF, FF_SPLIT):
        gate = _dot(h2, wgu_ref[:, c0:c0 + FF_SPLIT])
        up = _dot(h2, wgu_ref[:, D_FF + c0:D_FF + c0 + FF_SPLIT])
        act_scr[:, c0:c0 + FF_SPLIT] = (gate * _sigmoid(gate) * up).astype(BF16)
    o_ref[...] += _dot(act_scr[...], wd_ref[...])


def _merge_ffn(x, mab, gc, yc, wc, wo, fg, wgu, wd):
    M, D = x.shape
    tm = TM_FFN
    tok = lambda width: pl.BlockSpec((tm, width), lambda i: (i, 0))
    return pl.pallas_call(
        _merge_ffn_body,
        grid=(M // tm,),
        in_specs=[
            tok(D), tok(D), tok(D), tok(ATT_WIDTH),
            _resident(wc.shape), _resident(wo.shape), _resident(fg.shape),
            _resident(wgu.shape), _resident(wd.shape),
        ],
        out_specs=tok(D),
        out_shape=jax.ShapeDtypeStruct((M, D), F32),
        scratch_shapes=[pltpu.VMEM((tm, D_FF), BF16)],
        compiler_params=pltpu.CompilerParams(
            dimension_semantics=("arbitrary",),
            vmem_limit_bytes=VMEM_LIMIT_BYTES),
        name="merge_ffn",
    )(x, mab, gc, yc, wc, wo, fg, wgu, wd)


def kernel(x, mix_norm_g, w_in, b_gate, conv_w, conv_b, sgu_ln_g, sgu_ln_b, sgu_w, sgu_b,
           q_norm_g, k_norm_g, w_branch_out, w_o, ffn_norm_g, w_gate_up, w_down):
    B, S, D = x.shape
    depth = w_in.shape[0]
    row = lambda a: a.reshape(1, -1)
    for l in range(depth):
        sbf = jnp.repeat(sgu_b[l].T, CHUNK, axis=1)
        qg = jnp.tile(q_norm_g[l], LANES // HEAD_DIM).reshape(1, LANES)
        kg = jnp.tile(k_norm_g[l], LANES // HEAD_DIM).reshape(1, LANES)
        mab, gc, q, k, v = _mixer_in(
            x, row(mix_norm_g[l]), w_in[l].astype(BF16), row(b_gate[l]),
            conv_w[l], row(conv_b[l]), row(sgu_ln_g[l]), row(sgu_ln_b[l]),
            sgu_w[l], sbf, qg, kg, w_branch_out[l, 0:2].astype(BF16))
        yc = _attention(q, k, v)
        x = _merge_ffn(
            x.reshape(B * S, D), mab.reshape(B * S, D), gc.reshape(B * S, D),
            yc.reshape(B * S, ATT_WIDTH), w_branch_out[l, 2].astype(BF16),
            w_o[l].astype(BF16), row(ffn_norm_g[l]), w_gate_up[l].astype(BF16),
            w_down[l].astype(BF16)).reshape(B, S, D)
    return x
```

```python
import functools

import jax
import jax.numpy as jnp
from jax import lax
from jax.experimental import pallas as pl
from jax.experimental.pallas import tpu as pltpu

F32 = jnp.float32
BF16 = jnp.bfloat16

D_MODEL = 1024
CONV_CH = 512
CONV_K = 3
SGU_WIDTH = 512
SGU_GROUPS = 4
CHUNK = 128
ATT_HEADS = 8
HEAD_DIM = 64
ATT_WIDTH = ATT_HEADS * HEAD_DIM
N_BRANCH = 3
D_FF = 2816
IN_COLS = 7168
EPS = 1e-6

LANES = 128
SUBLANES = 8
VMEM_LIMIT_BYTES = 56 * 1024 * 1024

C_AB, C_AC, C_AX = 0, 512, 1024
C_SU, C_SV = 1536, 2048
C_Q, C_K, C_V = 2560, 3072, 3584
C_G = 4096

TM_IN = 512
TM_FFN = 512
TQ_OUTER = 1024
TQ = 128
TK = 256
FF_SPLIT = 1408

LOG2E = 1.4426950408889634

EXP2_ZERO_BOUND = -151.0


def _sigmoid(x):
    return 1.0 / (1.0 + jnp.exp(-x))


def _gelu_exact(x):
    return 0.5 * x * (1.0 + lax.erf(x * (2.0 ** -0.5)))


def _dot(a, b):
    return jnp.dot(a, b, preferred_element_type=F32)


def _resident(shape):
    zeros = (0,) * len(shape)
    return pl.BlockSpec(shape, lambda *_: zeros, pipeline_mode=pl.Buffered(1))


def _head_rmsnorm(p, g128):
    tm = p.shape[0]
    lo = lax.broadcasted_iota(jnp.int32, (tm, LANES), 1) < HEAD_DIM
    outs = []
    for c in range(ATT_WIDTH // LANES):
        pc = p[:, c * LANES:(c + 1) * LANES]
        pp = pc * pc
        s_lo = jnp.sum(jnp.where(lo, pp, 0.0), axis=-1, keepdims=True)
        s_hi = jnp.sum(jnp.where(lo, 0.0, pp), axis=-1, keepdims=True)
        ms = jnp.where(lo, s_lo, s_hi) * (1.0 / HEAD_DIM)
        outs.append(pc * lax.rsqrt(ms + EPS) * g128)
    return jnp.concatenate(outs, axis=-1)


def _mixer_in_body(x_ref, ng_ref, win_ref, bg_ref, cw_ref, cb_ref, lng_ref, lnb_ref,
                   sw_ref, sb_ref, qg_ref, kg_ref, wab_ref,
                   mab_ref, gc_ref, q_ref, k_ref, v_ref,
                   u_scr, yb_scr):
    tm = x_ref.shape[1]
    j = pl.program_id(1)

    x = x_ref[0]
    ms = jnp.mean(x * x, axis=-1, keepdims=True)
    h = (x * lax.rsqrt(ms + EPS) * ng_ref[...]).astype(BF16)

    def proj(c0, width):
        return _dot(h, win_ref[:, c0:c0 + width])

    @pl.when(j == 0)
    def _():
        u_scr[0:SUBLANES, :] = jnp.zeros((SUBLANES, CONV_CH), F32)

    u = proj(C_AC, CONV_CH) * proj(C_AX, CONV_CH)
    u_scr[SUBLANES:SUBLANES + tm, :] = u
    u1 = u_scr[SUBLANES - 1:SUBLANES - 1 + tm, :]
    u2 = u_scr[SUBLANES - 2:SUBLANES - 2 + tm, :]
    y = cb_ref[...] + cw_ref[0:1, :] * u2
    y = y + cw_ref[1:2, :] * u1
    y = y + cw_ref[2:3, :] * u
    ya = (proj(C_AB, CONV_CH) * y).astype(BF16)
    u_scr[0:SUBLANES, :] = u_scr[tm:tm + SUBLANES, :]

    gu = _gelu_exact(proj(C_SU, SGU_WIDTH))
    gv = _gelu_exact(proj(C_SV, SGU_WIDTH))
    mu = jnp.mean(gv, axis=-1, keepdims=True)
    vc = gv - mu
    var = jnp.mean(vc * vc, axis=-1, keepdims=True)
    vn = (vc * lax.rsqrt(var + EPS) * lng_ref[...] + lnb_ref[...]).astype(BF16)
    row = lax.broadcasted_iota(jnp.int32, (CHUNK, CHUNK), 0)
    col = lax.broadcasted_iota(jnp.int32, (CHUNK, CHUNK), 1)
    tril = col <= row
    for g in range(SGU_GROUPS):
        wg = jnp.where(tril, sw_ref[g], 0.0).astype(BF16)
        cs = slice(g * CHUNK, (g + 1) * CHUNK)
        for c in range(tm // CHUNK):
            rs = slice(c * CHUNK, (c + 1) * CHUNK)
            mixed = _dot(wg, vn[rs, cs]) + sb_ref[:, cs]
            yb_scr[rs, cs] = (gu[rs, cs] * mixed).astype(BF16)

    q = _head_rmsnorm(proj(C_Q, ATT_WIDTH), qg_ref[...]) * (HEAD_DIM ** -0.5 * LOG2E)
    q_ref[0] = q.astype(BF16)
    k_ref[0] = _head_rmsnorm(proj(C_K, ATT_WIDTH), kg_ref[...]).astype(BF16)
    v_ref[0] = proj(C_V, ATT_WIDTH).astype(BF16)

    ga = _sigmoid(proj(C_G, D_MODEL) + bg_ref[:, 0:D_MODEL])
    mab = ga * _dot(ya, wab_ref[0])
    gb = _sigmoid(proj(C_G + D_MODEL, D_MODEL) + bg_ref[:, D_MODEL:2 * D_MODEL])
    mab = mab + gb * _dot(yb_scr[...], wab_ref[1])
    mab_ref[0] = mab.astype(BF16)
    gc = _sigmoid(proj(C_G + 2 * D_MODEL, D_MODEL) + bg_ref[:, 2 * D_MODEL:3 * D_MODEL])
    gc_ref[0] = gc.astype(BF16)


def _mixer_in(x, ng, win, bg, cw, cb, lng, lnb, sw, sbf, qg, kg, wab):
    B, S, D = x.shape
    tm = TM_IN
    tok = lambda width: pl.BlockSpec((1, tm, width), lambda b, j: (b, j, 0))
    out_shape = (
        jax.ShapeDtypeStruct((B, S, D), BF16),
        jax.ShapeDtypeStruct((B, S, D), BF16),
        jax.ShapeDtypeStruct((B, S, ATT_WIDTH), BF16),
        jax.ShapeDtypeStruct((B, S, ATT_WIDTH), BF16),
        jax.ShapeDtypeStruct((B, S, ATT_WIDTH), BF16),
    )
    return pl.pallas_call(
        _mixer_in_body,
        grid=(B, S // tm),
        in_specs=[
            tok(D),
            _resident(ng.shape), _resident(win.shape), _resident(bg.shape),
            _resident(cw.shape), _resident(cb.shape), _resident(lng.shape),
            _resident(lnb.shape), _resident(sw.shape), _resident(sbf.shape),
            _resident(qg.shape), _resident(kg.shape), _resident(wab.shape),
        ],
        out_specs=(tok(D), tok(D), tok(ATT_WIDTH), tok(ATT_WIDTH), tok(ATT_WIDTH)),
        out_shape=out_shape,
        scratch_shapes=[
            pltpu.VMEM((tm + SUBLANES, CONV_CH), F32),
            pltpu.VMEM((tm, SGU_WIDTH), BF16),
        ],
        compiler_params=pltpu.CompilerParams(
            dimension_semantics=("arbitrary", "arbitrary"),
            vmem_limit_bytes=VMEM_LIMIT_BYTES),
        name="mixer_in",
    )(x, ng, win, bg, cw, cb, lng, lnb, sw, sbf, qg, kg, wab)


def _attention_body(q_ref, k_ref, v_ref, o_ref, acc_scr, carry_scr):
    io = pl.program_id(2)
    n_q = TQ_OUTER // TQ
    lo = lax.broadcasted_iota(jnp.int32, (TQ, LANES), 1) < HEAD_DIM

    def neg_inclusive(n):
        return jnp.where(lax.broadcasted_iota(jnp.int32, (n, n), 0)
                         >= lax.broadcasted_iota(jnp.int32, (n, n), 1), -1.0, 0.0).astype(BF16)

    incl_diag = neg_inclusive(TQ)
    incl_far = neg_inclusive(TK)
    diag_mask = (lax.broadcasted_iota(jnp.int32, (2 * TQ, TQ), 1)
                 < lax.broadcasted_iota(jnp.int32, (2 * TQ, TQ), 0) % TQ)
    far_col = lax.broadcasted_iota(jnp.int32, (2 * TQ, TK), 1)

    def stack_heads(q):
        zero = jnp.zeros_like(q)
        return jnp.concatenate([jnp.where(lo, q, zero), jnp.where(lo, zero, q)], axis=0)

    def key_block(q2, kblk, vblk, incl, carry, mask):
        z = lax.dot_general(q2, kblk, (((1,), (1,)), ((), ())), preferred_element_type=F32)
        sp = jnp.maximum(z, 0.0) + jnp.log(1.0 + jnp.exp2(-jnp.abs(z))) * LOG2E
        if mask is not None:
            sp = jnp.where(mask, sp, 0.0)
        neg_cum = _dot(sp.astype(BF16), incl)
        e = z + neg_cum
        if carry is not None:
            e = e + carry
        a = jnp.exp2(e)
        if mask is not None:
            a = jnp.where(mask, a, 0.0)
        return _dot(a.astype(BF16), vblk), neg_cum[:, 0:1]

    def write_out(q_rows, acc):
        o_ref[0, q_rows, :] = jnp.where(lo, acc[:TQ], acc[TQ:]).astype(o_ref.dtype)

    worst = None
    for qi in range(n_q):
        q_start = pl.multiple_of(io * TQ_OUTER + qi * TQ, TQ)
        far_start = pl.multiple_of(jnp.maximum(q_start - TK, 0), TQ)
        q2 = stack_heads(q_ref[0, qi * TQ:(qi + 1) * TQ, :])
        k_diag = k_ref[0, pl.ds(q_start, TQ), :]
        v_diag = v_ref[0, pl.ds(q_start, TQ), :]
        k_far = k_ref[0, pl.ds(far_start, TK), :]
        v_far = v_ref[0, pl.ds(far_start, TK), :]
        far_mask = (far_col + far_start) < q_start if qi * TQ < TK else None
        pv_d, c_d = key_block(q2, k_diag, v_diag, incl_diag, None, diag_mask)
        pv_f, c_f = key_block(q2, k_far, v_far, incl_far, c_d, far_mask)
        acc = pv_d + pv_f
        carry = c_d + c_f
        write_out(slice(qi * TQ, (qi + 1) * TQ), acc)
        acc_scr[qi] = acc
        carry_scr[qi] = jnp.broadcast_to(carry, (2 * TQ, LANES))
        worst = carry if worst is None else jnp.maximum(worst, carry)

    def finish(qi, _):
        q_rows = pl.ds(pl.multiple_of(qi * TQ, TQ), TQ)
        q2 = stack_heads(q_ref[0, q_rows, :])

        def unfinished():
            return jnp.max(carry_scr[qi]) > EXP2_ZERO_BOUND

        def cond(state):
            k_end, go = state
            return jnp.logical_and(k_end > 0, go)

        def body(state):
            k_end, _ = state
            k_start = pl.multiple_of(jnp.maximum(k_end - TK, 0), TQ)
            kblk = k_ref[0, pl.ds(k_start, TK), :]
            vblk = v_ref[0, pl.ds(k_start, TK), :]
            carry = carry_scr[qi]
            pv, c = key_block(q2, kblk, vblk, incl_far, carry[:, 0:1],
                              (far_col + k_start) < k_end)
            acc_scr[qi] += pv
            carry_scr[qi] = carry + c
            return k_end - TK, unfinished()

        lax.while_loop(cond, body, (io * TQ_OUTER + qi * TQ - TK, unfinished()))
        write_out(q_rows, acc_scr[qi])
        return 0

    @pl.when(jnp.max(worst) > EXP2_ZERO_BOUND)
    def _():
        lax.fori_loop(0, n_q, finish, 0)


def _attention(q, k, v):
    B, S, W = q.shape
    n_pairs = W // LANES
    qspec = pl.BlockSpec((1, TQ_OUTER, LANES), lambda b, p, i: (b, i, p))
    kvspec = pl.BlockSpec((1, S, LANES), lambda b, p, i: (b, 0, p))
    return pl.pallas_call(
        _attention_body,
        grid=(B, n_pairs, S // TQ_OUTER),
        in_specs=[qspec, kvspec, kvspec],
        out_specs=qspec,
        out_shape=jax.ShapeDtypeStruct((B, S, W), BF16),
        scratch_shapes=[
            pltpu.VMEM((TQ_OUTER // TQ, 2 * TQ, LANES), F32),
            pltpu.VMEM((TQ_OUTER // TQ, 2 * TQ, LANES), F32),
        ],
        compiler_params=pltpu.CompilerParams(
            dimension_semantics=("arbitrary", "arbitrary", "arbitrary"),
            vmem_limit_bytes=VMEM_LIMIT_BYTES),
        name="attention",
    )(q, k, v)


def _merge_ffn_body(x_ref, mab_ref, gc_ref, yc_ref, wc_ref, wo_ref, fg_ref, wgu_ref, wd_ref,
                    o_ref, act_scr):
    ydc = _dot(yc_ref[...], wc_ref[...])
    merged = mab_ref[...].astype(F32) + gc_ref[...].astype(F32) * ydc
    x1 = x_ref[...] + _dot(merged.astype(BF16), wo_ref[...])
    o_ref[...] = x1
    ms = jnp.mean(x1 * x1, axis=-1, keepdims=True)
    h2 = (x1 * lax.rsqrt(ms + EPS) * fg_ref[...]).astype(BF16)
    for c0 in range(0, D_FF, FF_SPLIT):
        gate = _dot(h2, wgu_ref[:, c0:c0 + FF_SPLIT])
        up = _dot(h2, wgu_ref[:, D_FF + c0:D_FF + c0 + FF_SPLIT])
        act_scr[:, c0:c0 + FF_SPLIT] = (gate * _sigmoid(gate) * up).astype(BF16)
    o_ref[...] += _dot(act_scr[...], wd_ref[...])


def _merge_ffn(x, mab, gc, yc, wc, wo, fg, wgu, wd):
    M, D = x.shape
    tm = TM_FFN
    tok = lambda width: pl.BlockSpec((tm, width), lambda i: (i, 0))
    return pl.pallas_call(
        _merge_ffn_body,
        grid=(M // tm,),
        in_specs=[
            tok(D), tok(D), tok(D), tok(ATT_WIDTH),
            _resident(wc.shape), _resident(wo.shape), _resident(fg.shape),
            _resident(wgu.shape), _resident(wd.shape),
        ],
        out_specs=tok(D),
        out_shape=jax.ShapeDtypeStruct((M, D), F32),
        scratch_shapes=[pltpu.VMEM((tm, D_FF), BF16)],
        compiler_params=pltpu.CompilerParams(
            dimension_semantics=("arbitrary",),
            vmem_limit_bytes=VMEM_LIMIT_BYTES),
        name="merge_ffn",
    )(x, mab, gc, yc, wc, wo, fg, wgu, wd)


def kernel(x, mix_norm_g, w_in, b_gate, conv_w, conv_b, sgu_ln_g, sgu_ln_b, sgu_w, sgu_b,
           q_norm_g, k_norm_g, w_branch_out, w_o, ffn_norm_g, w_gate_up, w_down):
    B, S, D = x.shape
    depth = w_in.shape[0]
    row = lambda a: a.reshape(1, -1)
    for l in range(depth):
        sbf = jnp.repeat(sgu_b[l].T, CHUNK, axis=1)
        qg = jnp.tile(q_norm_g[l], LANES // HEAD_DIM).reshape(1, LANES)
        kg = jnp.tile(k_norm_g[l], LANES // HEAD_DIM).reshape(1, LANES)
        mab, gc, q, k, v = _mixer_in(
            x, row(mix_norm_g[l]), w_in[l].astype(BF16), row(b_gate[l]),
            conv_w[l], row(conv_b[l]), row(sgu_ln_g[l]), row(sgu_ln_b[l]),
            sgu_w[l], sbf, qg, kg, w_branch_out[l, 0:2].astype(BF16))
        yc = _attention(q, k, v)
        x = _merge_ffn(
            x.reshape(B * S, D), mab.reshape(B * S, D), gc.reshape(B * S, D),
            yc.reshape(B * S, ATT_WIDTH), w_branch_out[l, 2].astype(BF16),
            w_o[l].astype(BF16), row(ffn_norm_g[l]), w_gate_up[l].astype(BF16),
            w_down[l].astype(BF16)).reshape(B, S, D)
    return x
```

```python
import functools

import jax
import jax.numpy as jnp
from jax import lax
from jax.experimental import pallas as pl
from jax.experimental.pallas import tpu as pltpu

F32 = jnp.float32
BF16 = jnp.bfloat16

D_MODEL = 1024
CONV_CH = 512
CONV_K = 3
SGU_WIDTH = 512
SGU_GROUPS = 4
CHUNK = 128
ATT_HEADS = 8
HEAD_DIM = 64
ATT_WIDTH = ATT_HEADS * HEAD_DIM
N_BRANCH = 3
D_FF = 2816
IN_COLS = 7168
EPS = 1e-6

LANES = 128
SUBLANES = 8
VMEM_LIMIT_BYTES = 56 * 1024 * 1024

C_AB, C_AC, C_AX = 0, 512, 1024
C_SU, C_SV = 1536, 2048
C_Q, C_K, C_V = 2560, 3072, 3584
C_G = 4096

TM_IN = 1024
TS_IN = 512
TM_FFN = 1024
TS_FFN = 512
TQ_OUTER = 1024
TQ = 128
TK = 256
MXU_DIM = 256
FF_SPLIT = 6 * MXU_DIM

LOG2E = 1.4426950408889634

EXP2_ZERO_BOUND = -151.0


def _sigmoid(x):
    return 1.0 / (1.0 + jnp.exp(-x))


def _gelu_exact(x):
    return 0.5 * x * (1.0 + lax.erf(x * (2.0 ** -0.5)))


def _dot(a, b):
    return jnp.dot(a, b, preferred_element_type=F32)


def _resident(shape):
    zeros = (0,) * len(shape)
    return pl.BlockSpec(shape, lambda *_: zeros, pipeline_mode=pl.Buffered(1))


def _head_rmsnorm(p, g128):
    tm = p.shape[0]
    lo = lax.broadcasted_iota(jnp.int32, (tm, LANES), 1) < HEAD_DIM
    outs = []
    for c in range(ATT_WIDTH // LANES):
        pc = p[:, c * LANES:(c + 1) * LANES]
        pp = pc * pc
        s_lo = jnp.sum(jnp.where(lo, pp, 0.0), axis=-1, keepdims=True)
        s_hi = jnp.sum(jnp.where(lo, 0.0, pp), axis=-1, keepdims=True)
        ms = jnp.where(lo, s_lo, s_hi) * (1.0 / HEAD_DIM)
        outs.append(pc * lax.rsqrt(ms + EPS) * g128)
    return jnp.concatenate(outs, axis=-1)


def _mixer_in_body(x_ref, ng_ref, win_ref, bg_ref, cw_ref, cb_ref, lng_ref, lnb_ref,
                   sw_ref, sb_ref, qg_ref, kg_ref, wab_ref,
                   mab_ref, gc_ref, q_ref, k_ref, v_ref,
                   u_scr, yb_scr):
    tm = x_ref.shape[1]
    ts = TS_IN
    j = pl.program_id(1)

    @pl.when(j == 0)
    def _():
        u_scr[0:SUBLANES, :] = jnp.zeros((SUBLANES, CONV_CH), F32)

    row = lax.broadcasted_iota(jnp.int32, (CHUNK, CHUNK), 0)
    col = lax.broadcasted_iota(jnp.int32, (CHUNK, CHUNK), 1)
    sgu_w = [jnp.where(col <= row, sw_ref[g], 0.0).astype(BF16) for g in range(SGU_GROUPS)]

    for r0 in range(0, tm, ts):
        rows = slice(r0, r0 + ts)
        x = x_ref[0, rows, :]
        ms = jnp.mean(x * x, axis=-1, keepdims=True)
        h = (x * lax.rsqrt(ms + EPS) * ng_ref[...]).astype(BF16)

        def proj(c0, width):
            return _dot(h, win_ref[:, c0:c0 + width])

        u = proj(C_AC, CONV_CH) * proj(C_AX, CONV_CH)
        u_scr[SUBLANES:SUBLANES + ts, :] = u
        u1 = u_scr[SUBLANES - 1:SUBLANES - 1 + ts, :]
        u2 = u_scr[SUBLANES - 2:SUBLANES - 2 + ts, :]
        y = cb_ref[...] + cw_ref[0:1, :] * u2
        y = y + cw_ref[1:2, :] * u1
        y = y + cw_ref[2:3, :] * u
        ya = (proj(C_AB, CONV_CH) * y).astype(BF16)
        u_scr[0:SUBLANES, :] = u_scr[ts:ts + SUBLANES, :]

        gu = _gelu_exact(proj(C_SU, SGU_WIDTH))
        gv = _gelu_exact(proj(C_SV, SGU_WIDTH))
        mu = jnp.mean(gv, axis=-1, keepdims=True)
        vc = gv - mu
        var = jnp.mean(vc * vc, axis=-1, keepdims=True)
        vn = (vc * lax.rsqrt(var + EPS) * lng_ref[...] + lnb_ref[...]).astype(BF16)
        for g in range(SGU_GROUPS):
            cs = slice(g * CHUNK, (g + 1) * CHUNK)
            for c in range(ts // CHUNK):
                rs = slice(c * CHUNK, (c + 1) * CHUNK)
                mixed = _dot(sgu_w[g], vn[rs, cs]) + sb_ref[:, cs]
                yb_scr[r0 + c * CHUNK:r0 + (c + 1) * CHUNK, cs] = (gu[rs, cs] * mixed).astype(BF16)

        q = _head_rmsnorm(proj(C_Q, ATT_WIDTH), qg_ref[...]) * (HEAD_DIM ** -0.5 * LOG2E)
        q_ref[0, rows, :] = q.astype(BF16)
        k_ref[0, rows, :] = _head_rmsnorm(proj(C_K, ATT_WIDTH), kg_ref[...]).astype(BF16)
        v_ref[0, rows, :] = proj(C_V, ATT_WIDTH).astype(BF16)

        ga = _sigmoid(proj(C_G, D_MODEL) + bg_ref[:, 0:D_MODEL])
        mab = ga * _dot(ya, wab_ref[0])
        gb = _sigmoid(proj(C_G + D_MODEL, D_MODEL) + bg_ref[:, D_MODEL:2 * D_MODEL])
        mab = mab + gb * _dot(yb_scr[rows, :], wab_ref[1])
        mab_ref[0, rows, :] = mab.astype(BF16)
        gc = _sigmoid(proj(C_G + 2 * D_MODEL, D_MODEL) + bg_ref[:, 2 * D_MODEL:3 * D_MODEL])
        gc_ref[0, rows, :] = gc.astype(BF16)


def _mixer_in(x, ng, win, bg, cw, cb, lng, lnb, sw, sbf, qg, kg, wab):
    B, S, D = x.shape
    tm = TM_IN
    tok = lambda width: pl.BlockSpec((1, tm, width), lambda b, j: (b, j, 0))
    out_shape = (
        jax.ShapeDtypeStruct((B, S, D), BF16),
        jax.ShapeDtypeStruct((B, S, D), BF16),
        jax.ShapeDtypeStruct((B, S, ATT_WIDTH), BF16),
        jax.ShapeDtypeStruct((B, S, ATT_WIDTH), BF16),
        jax.ShapeDtypeStruct((B, S, ATT_WIDTH), BF16),
    )
    return pl.pallas_call(
        _mixer_in_body,
        grid=(B, S // tm),
        in_specs=[
            tok(D),
            _resident(ng.shape), _resident(win.shape), _resident(bg.shape),
            _resident(cw.shape), _resident(cb.shape), _resident(lng.shape),
            _resident(lnb.shape), _resident(sw.shape), _resident(sbf.shape),
            _resident(qg.shape), _resident(kg.shape), _resident(wab.shape),
        ],
        out_specs=(tok(D), tok(D), tok(ATT_WIDTH), tok(ATT_WIDTH), tok(ATT_WIDTH)),
        out_shape=out_shape,
        scratch_shapes=[
            pltpu.VMEM((TS_IN + SUBLANES, CONV_CH), F32),
            pltpu.VMEM((tm, SGU_WIDTH), BF16),
        ],
        compiler_params=pltpu.CompilerParams(
            dimension_semantics=("arbitrary", "arbitrary"),
            vmem_limit_bytes=VMEM_LIMIT_BYTES),
        name="mixer_in",
    )(x, ng, win, bg, cw, cb, lng, lnb, sw, sbf, qg, kg, wab)


def _attention_body(q_ref, k_ref, v_ref, o_ref, acc_scr, carry_scr):
    io = pl.program_id(2)
    n_q = TQ_OUTER // TQ
    lo = lax.broadcasted_iota(jnp.int32, (TQ, LANES), 1) < HEAD_DIM

    def neg_inclusive(n):
        return jnp.where(lax.broadcasted_iota(jnp.int32, (n, n), 0)
                         >= lax.broadcasted_iota(jnp.int32, (n, n), 1), -1.0, 0.0).astype(BF16)

    incl_diag = neg_inclusive(TQ)
    incl_far = neg_inclusive(TK)
    diag_mask = (lax.broadcasted_iota(jnp.int32, (2 * TQ, TQ), 1)
                 < lax.broadcasted_iota(jnp.int32, (2 * TQ, TQ), 0) % TQ)
    far_col = lax.broadcasted_iota(jnp.int32, (2 * TQ, TK), 1)

    def stack_heads(q):
        zero = jnp.zeros_like(q)
        return jnp.concatenate([jnp.where(lo, q, zero), jnp.where(lo, zero, q)], axis=0)

    def key_block(q2, kblk, vblk, incl, carry, mask):
        z = lax.dot_general(q2, kblk, (((1,), (1,)), ((), ())), preferred_element_type=F32)
        sp = jnp.maximum(z, 0.0) + jnp.log(1.0 + jnp.exp2(-jnp.abs(z))) * LOG2E
        if mask is not None:
            sp = jnp.where(mask, sp, 0.0)
        neg_cum = _dot(sp.astype(BF16), incl)
        e = z + neg_cum
        if carry is not None:
            e = e + carry
        a = jnp.exp2(e)
        if mask is not None:
            a = jnp.where(mask, a, 0.0)
        return _dot(a.astype(BF16), vblk), neg_cum[:, 0:1]

    def write_out(q_rows, acc):
        o_ref[0, q_rows, :] = jnp.where(lo, acc[:TQ], acc[TQ:]).astype(o_ref.dtype)

    worst = None
    for qi in range(n_q):
        q_start = pl.multiple_of(io * TQ_OUTER + qi * TQ, TQ)
        far_start = pl.multiple_of(jnp.maximum(q_start - TK, 0), TQ)
        q2 = stack_heads(q_ref[0, qi * TQ:(qi + 1) * TQ, :])
        k_diag = k_ref[0, pl.ds(q_start, TQ), :]
        v_diag = v_ref[0, pl.ds(q_start, TQ), :]
        k_far = k_ref[0, pl.ds(far_start, TK), :]
        v_far = v_ref[0, pl.ds(far_start, TK), :]
        far_mask = (far_col + far_start) < q_start if qi * TQ < TK else None
        pv_d, c_d = key_block(q2, k_diag, v_diag, incl_diag, None, diag_mask)
        pv_f, c_f = key_block(q2, k_far, v_far, incl_far, c_d, far_mask)
        acc = pv_d + pv_f
        carry = c_d + c_f
        write_out(slice(qi * TQ, (qi + 1) * TQ), acc)
        acc_scr[qi] = acc
        carry_scr[qi] = jnp.broadcast_to(carry, (2 * TQ, LANES))
        worst = carry if worst is None else jnp.maximum(worst, carry)

    def finish(qi, _):
        q_rows = pl.ds(pl.multiple_of(qi * TQ, TQ), TQ)
        q2 = stack_heads(q_ref[0, q_rows, :])

        def unfinished():
            return jnp.max(carry_scr[qi]) > EXP2_ZERO_BOUND

        def cond(state):
            k_end, go = state
            return jnp.logical_and(k_end > 0, go)

        def body(state):
            k_end, _ = state
            k_start = pl.multiple_of(jnp.maximum(k_end - TK, 0), TQ)
            kblk = k_ref[0, pl.ds(k_start, TK), :]
            vblk = v_ref[0, pl.ds(k_start, TK), :]
            carry = carry_scr[qi]
            pv, c = key_block(q2, kblk, vblk, incl_far, carry[:, 0:1],
                              (far_col + k_start) < k_end)
            acc_scr[qi] += pv
            carry_scr[qi] = carry + c
            return k_end - TK, unfinished()

        lax.while_loop(cond, body, (io * TQ_OUTER + qi * TQ - TK, unfinished()))
        write_out(q_rows, acc_scr[qi])
        return 0

    @pl.when(jnp.max(worst) > EXP2_ZERO_BOUND)
    def _():
        lax.fori_loop(0, n_q, finish, 0)


def _attention(q, k, v):
    B, S, W = q.shape
    n_pairs = W // LANES
    qspec = pl.BlockSpec((1, TQ_OUTER, LANES), lambda b, p, i: (b, i, p))
    kvspec = pl.BlockSpec((1, S, LANES), lambda b, p, i: (b, 0, p))
    return pl.pallas_call(
        _attention_body,
        grid=(B, n_pairs, S // TQ_OUTER),
        in_specs=[qspec, kvspec, kvspec],
        out_specs=qspec,
        out_shape=jax.ShapeDtypeStruct((B, S, W), BF16),
        scratch_shapes=[
            pltpu.VMEM((TQ_OUTER // TQ, 2 * TQ, LANES), F32),
            pltpu.VMEM((TQ_OUTER // TQ, 2 * TQ, LANES), F32),
        ],
        compiler_params=pltpu.CompilerParams(
            dimension_semantics=("arbitrary", "arbitrary", "arbitrary"),
            vmem_limit_bytes=VMEM_LIMIT_BYTES),
        name="attention",
    )(q, k, v)


def _merge_ffn_body(x_ref, mab_ref, gc_ref, yc_ref, wc_ref, wo_ref, fg_ref, wgu_ref, wd_ref,
                    o_ref, act_scr):
    for r0 in range(0, x_ref.shape[0], TS_FFN):
        rows = slice(r0, r0 + TS_FFN)
        ydc = _dot(yc_ref[rows, :], wc_ref[...])
        merged = mab_ref[rows, :].astype(F32) + gc_ref[rows, :].astype(F32) * ydc
        x1 = x_ref[rows, :] + _dot(merged.astype(BF16), wo_ref[...])
        o_ref[rows, :] = x1
        ms = jnp.mean(x1 * x1, axis=-1, keepdims=True)
        h2 = (x1 * lax.rsqrt(ms + EPS) * fg_ref[...]).astype(BF16)
        for c0 in range(0, D_FF, FF_SPLIT):
            c1 = min(c0 + FF_SPLIT, D_FF)
            gate = _dot(h2, wgu_ref[:, c0:c1])
            up = _dot(h2, wgu_ref[:, D_FF + c0:D_FF + c1])
            act_scr[rows, c0:c1] = (gate * _sigmoid(gate) * up).astype(BF16)
        o_ref[rows, :] += _dot(act_scr[rows, :], wd_ref[...])


def _merge_ffn(x, mab, gc, yc, wc, wo, fg, wgu, wd):
    M, D = x.shape
    tm = TM_FFN
    tok = lambda width: pl.BlockSpec((tm, width), lambda i: (i, 0))
    return pl.pallas_call(
        _merge_ffn_body,
        grid=(M // tm,),
        in_specs=[
            tok(D), tok(D), tok(D), tok(ATT_WIDTH),
            _resident(wc.shape), _resident(wo.shape), _resident(fg.shape),
            _resident(wgu.shape), _resident(wd.shape),
        ],
        out_specs=tok(D),
        out_shape=jax.ShapeDtypeStruct((M, D), F32),
        scratch_shapes=[pltpu.VMEM((tm, D_FF), BF16)],
        compiler_params=pltpu.CompilerParams(
            dimension_semantics=("arbitrary",),
            vmem_limit_bytes=VMEM_LIMIT_BYTES),
        name="merge_ffn",
    )(x, mab, gc, yc, wc, wo, fg, wgu, wd)


def kernel(x, mix_norm_g, w_in, b_gate, conv_w, conv_b, sgu_ln_g, sgu_ln_b, sgu_w, sgu_b,
           q_norm_g, k_norm_g, w_branch_out, w_o, ffn_norm_g, w_gate_up, w_down):
    B, S, D = x.shape
    depth = w_in.shape[0]
    row = lambda a: a.reshape(1, -1)
    for l in range(depth):
        sbf = jnp.repeat(sgu_b[l].T, CHUNK, axis=1)
        qg = jnp.tile(q_norm_g[l], LANES // HEAD_DIM).reshape(1, LANES)
        kg = jnp.tile(k_norm_g[l], LANES // HEAD_DIM).reshape(1, LANES)
        mab, gc, q, k, v = _mixer_in(
            x, row(mix_norm_g[l]), w_in[l].astype(BF16), row(b_gate[l]),
            conv_w[l], row(conv_b[l]), row(sgu_ln_g[l]), row(sgu_ln_b[l]),
            sgu_w[l], sbf, qg, kg, w_branch_out[l, 0:2].astype(BF16))
        yc = _attention(q, k, v)
        x = _merge_ffn(
            x.reshape(B * S, D), mab.reshape(B * S, D), gc.reshape(B * S, D),
            yc.reshape(B * S, ATT_WIDTH), w_branch_out[l, 2].astype(BF16),
            w_o[l].astype(BF16), row(ffn_norm_g[l]), w_gate_up[l].astype(BF16),
            w_down[l].astype(BF16)).reshape(B, S, D)
    return x
```

```python
import functools

import jax
import jax.numpy as jnp
from jax import lax
from jax.experimental import pallas as pl
from jax.experimental.pallas import tpu as pltpu

F32 = jnp.float32
BF16 = jnp.bfloat16

D_MODEL = 1024
CONV_CH = 512
CONV_K = 3
SGU_WIDTH = 512
SGU_GROUPS = 4
CHUNK = 128
ATT_HEADS = 8
HEAD_DIM = 64
ATT_WIDTH = ATT_HEADS * HEAD_DIM
N_BRANCH = 3
D_FF = 2816
IN_COLS = 7168
EPS = 1e-6

LANES = 128
SUBLANES = 8
VMEM_LIMIT_BYTES = 56 * 1024 * 1024

C_AB, C_AC, C_AX = 0, 512, 1024
C_SU, C_SV = 1536, 2048
C_Q, C_K, C_V = 2560, 3072, 3584
C_G = 4096

TM_IN = 1024
TS_IN = 512
TM_FFN = 1024
TS_FFN = 512
TQ_OUTER = 2048
TQ = 128
TK = 256
MXU_DIM = 256
FF_SPLIT = 6 * MXU_DIM

LOG2E = 1.4426950408889634
EXP2_CLAMP = 126.0
STAGE_SKEW = 2

EXP2_ZERO_BOUND = -151.0


def _sigmoid(x):
    return 1.0 / (1.0 + jnp.exp(-x))


def _gelu_exact(x):
    return 0.5 * x * (1.0 + lax.erf(x * (2.0 ** -0.5)))


def _dot(a, b):
    return jnp.dot(a, b, preferred_element_type=F32)


def _resident(shape):
    zeros = (0,) * len(shape)
    return pl.BlockSpec(shape, lambda *_: zeros, pipeline_mode=pl.Buffered(1))


def _head_rmsnorm(p, g128):
    tm = p.shape[0]
    lo = lax.broadcasted_iota(jnp.int32, (tm, LANES), 1) < HEAD_DIM
    outs = []
    for c in range(ATT_WIDTH // LANES):
        pc = p[:, c * LANES:(c + 1) * LANES]
        pp = pc * pc
        s_lo = jnp.sum(jnp.where(lo, pp, 0.0), axis=-1, keepdims=True)
        s_hi = jnp.sum(jnp.where(lo, 0.0, pp), axis=-1, keepdims=True)
        ms = jnp.where(lo, s_lo, s_hi) * (1.0 / HEAD_DIM)
        outs.append(pc * lax.rsqrt(ms + EPS) * g128)
    return jnp.concatenate(outs, axis=-1)


def _mixer_in_body(x_ref, ng_ref, win_ref, bg_ref, cw_ref, cb_ref, lng_ref, lnb_ref,
                   sw_ref, sb_ref, qg_ref, kg_ref, wab_ref,
                   mab_ref, gc_ref, q_ref, k_ref, v_ref,
                   u_scr, yb_scr):
    tm = x_ref.shape[1]
    ts = TS_IN
    j = pl.program_id(1)

    @pl.when(j == 0)
    def _():
        u_scr[0:SUBLANES, :] = jnp.zeros((SUBLANES, CONV_CH), F32)

    row = lax.broadcasted_iota(jnp.int32, (CHUNK, CHUNK), 0)
    col = lax.broadcasted_iota(jnp.int32, (CHUNK, CHUNK), 1)
    sgu_w = [jnp.where(col <= row, sw_ref[g], 0.0).astype(BF16) for g in range(SGU_GROUPS)]

    def proj(st, c0, width):
        return _dot(st["h"], win_ref[:, c0:c0 + width])

    def stage_norm(st):
        x = x_ref[0, st["rows"], :]
        ms = jnp.mean(x * x, axis=-1, keepdims=True)
        st["h"] = (x * lax.rsqrt(ms + EPS) * ng_ref[...]).astype(BF16)

    def stage_conv(st):
        u = proj(st, C_AC, CONV_CH) * proj(st, C_AX, CONV_CH)
        u_scr[SUBLANES:SUBLANES + ts, :] = u
        u1 = u_scr[SUBLANES - 1:SUBLANES - 1 + ts, :]
        u2 = u_scr[SUBLANES - 2:SUBLANES - 2 + ts, :]
        y = cb_ref[...] + cw_ref[0:1, :] * u2
        y = y + cw_ref[1:2, :] * u1
        y = y + cw_ref[2:3, :] * u
        st["ya"] = (proj(st, C_AB, CONV_CH) * y).astype(BF16)
        u_scr[0:SUBLANES, :] = u_scr[ts:ts + SUBLANES, :]

    def stage_sgu_in(st):
        st["gu"] = _gelu_exact(proj(st, C_SU, SGU_WIDTH))
        gv = _gelu_exact(proj(st, C_SV, SGU_WIDTH))
        mu = jnp.mean(gv, axis=-1, keepdims=True)
        vc = gv - mu
        var = jnp.mean(vc * vc, axis=-1, keepdims=True)
        st["vn"] = (vc * lax.rsqrt(var + EPS) * lng_ref[...] + lnb_ref[...]).astype(BF16)

    def stage_qkv(st):
        rows = st["rows"]
        q = _head_rmsnorm(proj(st, C_Q, ATT_WIDTH), qg_ref[...]) * (HEAD_DIM ** -0.5 * LOG2E)
        q_ref[0, rows, :] = q.astype(BF16)
        k_ref[0, rows, :] = _head_rmsnorm(proj(st, C_K, ATT_WIDTH), kg_ref[...]).astype(BF16)
        v_ref[0, rows, :] = proj(st, C_V, ATT_WIDTH).astype(BF16)

    def stage_sgu_mix(st):
        r0 = st["rows"].start
        for g in range(SGU_GROUPS):
            cs = slice(g * CHUNK, (g + 1) * CHUNK)
            for c in range(ts // CHUNK):
                rs = slice(c * CHUNK, (c + 1) * CHUNK)
                mixed = _dot(sgu_w[g], st["vn"][rs, cs]) + sb_ref[:, cs]
                yb_scr[r0 + c * CHUNK:r0 + (c + 1) * CHUNK, cs] = (
                    st["gu"][rs, cs] * mixed).astype(BF16)

    def stage_gates(st):
        rows = st["rows"]
        ga = _sigmoid(proj(st, C_G, D_MODEL) + bg_ref[:, 0:D_MODEL])
        gb = _sigmoid(proj(st, C_G + D_MODEL, D_MODEL) + bg_ref[:, D_MODEL:2 * D_MODEL])
        mab = ga * _dot(st["ya"], wab_ref[0]) + gb * _dot(yb_scr[rows, :], wab_ref[1])
        mab_ref[0, rows, :] = mab.astype(BF16)
        gc = _sigmoid(proj(st, C_G + 2 * D_MODEL, D_MODEL) + bg_ref[:, 2 * D_MODEL:3 * D_MODEL])
        gc_ref[0, rows, :] = gc.astype(BF16)

    subs = [dict(rows=slice(r0, r0 + ts)) for r0 in range(0, tm, ts)]
    for st in subs:
        stage_norm(st)
    for i in range(len(subs) + 1):
        if i < len(subs):
            stage_conv(subs[i])
            stage_sgu_in(subs[i])
            stage_qkv(subs[i])
        if i >= 1:
            stage_sgu_mix(subs[i - 1])
            stage_gates(subs[i - 1])


def _mixer_in(x, ng, win, bg, cw, cb, lng, lnb, sw, sbf, qg, kg, wab):
    B, S, D = x.shape
    tm = TM_IN
    tok = lambda width: pl.BlockSpec((1, tm, width), lambda b, j: (b, j, 0))
    out_shape = (
        jax.ShapeDtypeStruct((B, S, D), BF16),
        jax.ShapeDtypeStruct((B, S, D), BF16),
        jax.ShapeDtypeStruct((B, S, ATT_WIDTH), BF16),
        jax.ShapeDtypeStruct((B, S, ATT_WIDTH), BF16),
        jax.ShapeDtypeStruct((B, S, ATT_WIDTH), BF16),
    )
    return pl.pallas_call(
        _mixer_in_body,
        grid=(B, S // tm),
        in_specs=[
            tok(D),
            _resident(ng.shape), _resident(win.shape), _resident(bg.shape),
            _resident(cw.shape), _resident(cb.shape), _resident(lng.shape),
            _resident(lnb.shape), _resident(sw.shape), _resident(sbf.shape),
            _resident(qg.shape), _resident(kg.shape), _resident(wab.shape),
        ],
        out_specs=(tok(D), tok(D), tok(ATT_WIDTH), tok(ATT_WIDTH), tok(ATT_WIDTH)),
        out_shape=out_shape,
        scratch_shapes=[
            pltpu.VMEM((TS_IN + SUBLANES, CONV_CH), F32),
            pltpu.VMEM((tm, SGU_WIDTH), BF16),
        ],
        compiler_params=pltpu.CompilerParams(
            dimension_semantics=("arbitrary", "arbitrary"),
            vmem_limit_bytes=VMEM_LIMIT_BYTES),
        name="mixer_in",
    )(x, ng, win, bg, cw, cb, lng, lnb, sw, sbf, qg, kg, wab)


def _attention_body(q_ref, k_ref, v_ref, o_ref, acc_scr, carry_scr):
    io = pl.program_id(2)
    n_q = TQ_OUTER // TQ
    lo = lax.broadcasted_iota(jnp.int32, (TQ, LANES), 1) < HEAD_DIM

    def neg_inclusive(n):
        return jnp.where(lax.broadcasted_iota(jnp.int32, (n, n), 0)
                         >= lax.broadcasted_iota(jnp.int32, (n, n), 1), -1.0, 0.0).astype(BF16)

    incl_diag = neg_inclusive(TQ)
    incl_far = neg_inclusive(TK)
    diag_mask = (lax.broadcasted_iota(jnp.int32, (2 * TQ, TQ), 1)
                 < lax.broadcasted_iota(jnp.int32, (2 * TQ, TQ), 0) % TQ)
    far_col = lax.broadcasted_iota(jnp.int32, (2 * TQ, TK), 1)

    def stack_heads(q):
        zero = jnp.zeros_like(q)
        return jnp.concatenate([jnp.where(lo, q, zero), jnp.where(lo, zero, q)], axis=0)

    def logits(q2, kblk, mask):
        z = lax.dot_general(q2, kblk, (((1,), (1,)), ((), ())), preferred_element_type=F32)
        sp = jnp.maximum(z, jnp.log(1.0 + jnp.exp2(jnp.minimum(z, EXP2_CLAMP))) * LOG2E)
        if mask is not None:
            sp = jnp.where(mask, sp, 0.0)
        return z, sp.astype(BF16)

    def weights(z, sp, incl, carry, mask):
        neg_cum = _dot(sp, incl)
        e = z + neg_cum
        if carry is not None:
            e = e + carry
        a = jnp.exp2(e)
        if mask is not None:
            a = jnp.where(mask, a, 0.0)
        return a.astype(BF16), neg_cum[:, 0:1]

    def key_block(q2, kblk, vblk, incl, carry, mask):
        z, sp = logits(q2, kblk, mask)
        a, c = weights(z, sp, incl, carry, mask)
        return _dot(a, vblk), c

    def write_out(q_rows, acc):
        o_ref[0, q_rows, :] = jnp.where(lo, acc[:TQ], acc[TQ:]).astype(o_ref.dtype)

    def stage_logits(qi):
        q_start = pl.multiple_of(io * TQ_OUTER + qi * TQ, TQ)
        far_start = pl.multiple_of(jnp.maximum(q_start - TK, 0), TQ)
        q2 = stack_heads(q_ref[0, qi * TQ:(qi + 1) * TQ, :])
        far_mask = (far_col + far_start) < q_start if qi * TQ < TK else None
        z_d, sp_d = logits(q2, k_ref[0, pl.ds(q_start, TQ), :], diag_mask)
        z_f, sp_f = logits(q2, k_ref[0, pl.ds(far_start, TK), :], far_mask)
        return dict(q_start=q_start, far_start=far_start, far_mask=far_mask,
                    z_d=z_d, sp_d=sp_d, z_f=z_f, sp_f=sp_f)

    def stage_weights(st):
        a_d, c_d = weights(st["z_d"], st["sp_d"], incl_diag, None, diag_mask)
        a_f, c_f = weights(st["z_f"], st["sp_f"], incl_far, c_d, st["far_mask"])
        return dict(q_start=st["q_start"], far_start=st["far_start"],
                    a_d=a_d, a_f=a_f, carry=c_d + c_f)

    def stage_values(st):
        pv_d = _dot(st["a_d"], v_ref[0, pl.ds(st["q_start"], TQ), :])
        pv_f = _dot(st["a_f"], v_ref[0, pl.ds(st["far_start"], TK), :])
        return pv_d + pv_f, st["carry"]

    after_logits, after_weights, accs, carries = {}, {}, [], []
    for t in range(n_q + 2 * STAGE_SKEW):
        if t < n_q:
            after_logits[t] = stage_logits(t)
        if 0 <= t - STAGE_SKEW < n_q:
            after_weights[t - STAGE_SKEW] = stage_weights(after_logits.pop(t - STAGE_SKEW))
        if 0 <= t - 2 * STAGE_SKEW < n_q:
            acc, carry = stage_values(after_weights.pop(t - 2 * STAGE_SKEW))
            accs.append(acc)
            carries.append(carry)

    worst = functools.reduce(jnp.maximum, carries)
    for qi in range(n_q):
        write_out(slice(qi * TQ, (qi + 1) * TQ), accs[qi])
        acc_scr[qi] = accs[qi]
        carry_scr[qi] = jnp.broadcast_to(carries[qi], (2 * TQ, LANES))

    def finish(qi, _):
        q_rows = pl.ds(pl.multiple_of(qi * TQ, TQ), TQ)
        q2 = stack_heads(q_ref[0, q_rows, :])

        def unfinished():
            return jnp.max(carry_scr[qi]) > EXP2_ZERO_BOUND

        def cond(state):
            k_end, go = state
            return jnp.logical_and(k_end > 0, go)

        def body(state):
            k_end, _ = state
            k_start = pl.multiple_of(jnp.maximum(k_end - TK, 0), TQ)
            kblk = k_ref[0, pl.ds(k_start, TK), :]
            vblk = v_ref[0, pl.ds(k_start, TK), :]
            carry = carry_scr[qi]
            pv, c = key_block(q2, kblk, vblk, incl_far, carry[:, 0:1],
                              (far_col + k_start) < k_end)
            acc_scr[qi] += pv
            carry_scr[qi] = carry + c
            return k_end - TK, unfinished()

        lax.while_loop(cond, body, (io * TQ_OUTER + qi * TQ - TK, unfinished()))
        write_out(q_rows, acc_scr[qi])
        return 0

    @pl.when(jnp.max(worst) > EXP2_ZERO_BOUND)
    def _():
        lax.fori_loop(0, n_q, finish, 0)


def _attention(q, k, v):
    B, S, W = q.shape
    n_pairs = W // LANES
    qspec = pl.BlockSpec((1, TQ_OUTER, LANES), lambda b, p, i: (b, i, p))
    kvspec = pl.BlockSpec((1, S, LANES), lambda b, p, i: (b, 0, p))
    return pl.pallas_call(
        _attention_body,
        grid=(B, n_pairs, S // TQ_OUTER),
        in_specs=[qspec, kvspec, kvspec],
        out_specs=qspec,
        out_shape=jax.ShapeDtypeStruct((B, S, W), BF16),
        scratch_shapes=[
            pltpu.VMEM((TQ_OUTER // TQ, 2 * TQ, LANES), F32),
            pltpu.VMEM((TQ_OUTER // TQ, 2 * TQ, LANES), F32),
        ],
        compiler_params=pltpu.CompilerParams(
            dimension_semantics=("arbitrary", "arbitrary", "arbitrary"),
            vmem_limit_bytes=VMEM_LIMIT_BYTES),
        name="attention",
    )(q, k, v)


def _merge_ffn_body(x_ref, mab_ref, gc_ref, yc_ref, wc_ref, wo_ref, fg_ref, wgu_ref, wd_ref,
                    o_ref, act_scr):
    def stage_branch(st):
        st["ydc"] = _dot(yc_ref[st["rows"], :], wc_ref[...])

    def stage_wo(st):
        rows = st["rows"]
        merged = mab_ref[rows, :].astype(F32) + gc_ref[rows, :].astype(F32) * st["ydc"]
        x1 = x_ref[rows, :] + _dot(merged.astype(BF16), wo_ref[...])
        o_ref[rows, :] = x1
        ms = jnp.mean(x1 * x1, axis=-1, keepdims=True)
        st["h2"] = (x1 * lax.rsqrt(ms + EPS) * fg_ref[...]).astype(BF16)

    def stage_gate_up(st, c0):
        c1 = min(c0 + FF_SPLIT, D_FF)
        gate = _dot(st["h2"], wgu_ref[:, c0:c1])
        up = _dot(st["h2"], wgu_ref[:, D_FF + c0:D_FF + c1])
        act_scr[st["rows"], c0:c1] = (gate * _sigmoid(gate) * up).astype(BF16)

    def stage_down(st):
        rows = st["rows"]
        o_ref[rows, :] += _dot(act_scr[rows, :], wd_ref[...])

    subs = [dict(rows=slice(r0, r0 + TS_FFN)) for r0 in range(0, x_ref.shape[0], TS_FFN)]
    for stage in (stage_branch, stage_wo):
        for st in subs:
            stage(st)
    for c0 in range(0, D_FF, FF_SPLIT):
        for st in subs:
            stage_gate_up(st, c0)
    for st in subs:
        stage_down(st)


def _merge_ffn(x, mab, gc, yc, wc, wo, fg, wgu, wd):
    M, D = x.shape
    tm = TM_FFN
    tok = lambda width: pl.BlockSpec((tm, width), lambda i: (i, 0))
    return pl.pallas_call(
        _merge_ffn_body,
        grid=(M // tm,),
        in_specs=[
            tok(D), tok(D), tok(D), tok(ATT_WIDTH),
            _resident(wc.shape), _resident(wo.shape), _resident(fg.shape),
            _resident(wgu.shape), _resident(wd.shape),
        ],
        out_specs=tok(D),
        out_shape=jax.ShapeDtypeStruct((M, D), F32),
        scratch_shapes=[pltpu.VMEM((tm, D_FF), BF16)],
        compiler_params=pltpu.CompilerParams(
            dimension_semantics=("arbitrary",),
            vmem_limit_bytes=VMEM_LIMIT_BYTES),
        name="merge_ffn",
    )(x, mab, gc, yc, wc, wo, fg, wgu, wd)


def kernel(x, mix_norm_g, w_in, b_gate, conv_w, conv_b, sgu_ln_g, sgu_ln_b, sgu_w, sgu_b,
           q_norm_g, k_norm_g, w_branch_out, w_o, ffn_norm_g, w_gate_up, w_down):
    B, S, D = x.shape
    depth = w_in.shape[0]
    row = lambda a: a.reshape(1, -1)
    for l in range(depth):
        sbf = jnp.repeat(sgu_b[l].T, CHUNK, axis=1)
        qg = jnp.tile(q_norm_g[l], LANES // HEAD_DIM).reshape(1, LANES)
        kg = jnp.tile(k_norm_g[l], LANES // HEAD_DIM).reshape(1, LANES)
        mab, gc, q, k, v = _mixer_in(
            x, row(mix_norm_g[l]), w_in[l].astype(BF16), row(b_gate[l]),
            conv_w[l], row(conv_b[l]), row(sgu_ln_g[l]), row(sgu_ln_b[l]),
            sgu_w[l], sbf, qg, kg, w_branch_out[l, 0:2].astype(BF16))
        yc = _attention(q, k, v)
        x = _merge_ffn(
            x.reshape(B * S, D), mab.reshape(B * S, D), gc.reshape(B * S, D),
            yc.reshape(B * S, ATT_WIDTH), w_branch_out[l, 2].astype(BF16),
            w_o[l].astype(BF16), row(ffn_norm_g[l]), w_gate_up[l].astype(BF16),
            w_down[l].astype(BF16)).reshape(B, S, D)
    return x
```

```python
import functools

import jax
import jax.numpy as jnp
from jax import lax
from jax.experimental import pallas as pl
from jax.experimental.pallas import tpu as pltpu

F32 = jnp.float32
BF16 = jnp.bfloat16

D_MODEL = 1024
CONV_CH = 512
CONV_K = 3
SGU_WIDTH = 512
SGU_GROUPS = 4
CHUNK = 128
ATT_HEADS = 8
HEAD_DIM = 64
ATT_WIDTH = ATT_HEADS * HEAD_DIM
N_BRANCH = 3
D_FF = 2816
IN_COLS = 7168
EPS = 1e-6

LANES = 128
SUBLANES = 8
VMEM_LIMIT_BYTES = 56 * 1024 * 1024

C_AB, C_AC, C_AX = 0, 512, 1024
C_SU, C_SV = 1536, 2048
C_Q, C_K, C_V = 2560, 3072, 3584
C_G = 4096

TM_IN = 1024
TS_IN = 512
TM_FFN = 1024
TS_FFN = 512
TQ_OUTER = 2048
TQ = 128
TK = 256
MXU_DIM = 256
FF_SPLIT = 6 * MXU_DIM
CAST_BLOCK_BYTES = 8 * 1024 * 1024

LOG2E = 1.4426950408889634
EXP2_CLAMP = 126.0
STAGE_SKEW = 2

EXP2_ZERO_BOUND = -151.0


def _sigmoid(x):
    return 1.0 / (1.0 + jnp.exp(-x))


def _gelu_exact(x):
    return 0.5 * x * (1.0 + lax.erf(x * (2.0 ** -0.5)))


def _dot(a, b):
    return jnp.dot(a, b, preferred_element_type=F32)


def _resident(shape, lead=()):
    block = (None,) * len(lead) + tuple(shape[len(lead):])
    index = tuple(lead) + (0,) * (len(shape) - len(lead))
    return pl.BlockSpec(block, lambda *_: index, pipeline_mode=pl.Buffered(1))


def _cast_body(w_ref, o_ref):
    o_ref[...] = w_ref[...].astype(o_ref.dtype)


def _cast_rows(n_rows, n_cols):
    best = None
    for rows in range(2 * SUBLANES, n_rows + 1, 2 * SUBLANES):
        if n_rows % rows == 0 and rows * n_cols * 4 <= CAST_BLOCK_BYTES:
            best = rows
    assert best is not None, (n_rows, n_cols)
    return best


def _cast_bf16(w):
    L, R, C = w.shape
    rows = _cast_rows(R, C)
    spec = pl.BlockSpec((1, rows, C), lambda l, r: (l, r, 0))
    return pl.pallas_call(
        _cast_body,
        grid=(L, R // rows),
        in_specs=[spec],
        out_specs=spec,
        out_shape=jax.ShapeDtypeStruct(w.shape, BF16),
        compiler_params=pltpu.CompilerParams(
            dimension_semantics=("arbitrary", "arbitrary"),
            vmem_limit_bytes=VMEM_LIMIT_BYTES),
        name="cast_bf16",
    )(w)


def _head_rmsnorm(p, g128):
    tm = p.shape[0]
    lo = lax.broadcasted_iota(jnp.int32, (tm, LANES), 1) < HEAD_DIM
    outs = []
    for c in range(ATT_WIDTH // LANES):
        pc = p[:, c * LANES:(c + 1) * LANES]
        pp = pc * pc
        s_lo = jnp.sum(jnp.where(lo, pp, 0.0), axis=-1, keepdims=True)
        s_hi = jnp.sum(jnp.where(lo, 0.0, pp), axis=-1, keepdims=True)
        ms = jnp.where(lo, s_lo, s_hi) * (1.0 / HEAD_DIM)
        outs.append(pc * lax.rsqrt(ms + EPS) * g128)
    return jnp.concatenate(outs, axis=-1)


def _mixer_in_body(x_ref, ng_ref, win_ref, bg_ref, cw_ref, cb_ref, lng_ref, lnb_ref,
                   sw_ref, sb_ref, qg_ref, kg_ref, wab_ref,
                   mab_ref, gc_ref, q_ref, k_ref, v_ref,
                   u_scr, yb_scr):
    tm = x_ref.shape[1]
    ts = TS_IN
    j = pl.program_id(1)

    @pl.when(j == 0)
    def _():
        u_scr[0:SUBLANES, :] = jnp.zeros((SUBLANES, CONV_CH), F32)

    row = lax.broadcasted_iota(jnp.int32, (CHUNK, CHUNK), 0)
    col = lax.broadcasted_iota(jnp.int32, (CHUNK, CHUNK), 1)
    sgu_w = [jnp.where(col <= row, sw_ref[g], 0.0).astype(BF16) for g in range(SGU_GROUPS)]

    def proj(st, c0, width):
        return _dot(st["h"], win_ref[:, c0:c0 + width])

    def stage_norm(st):
        x = x_ref[0, st["rows"], :]
        ms = jnp.mean(x * x, axis=-1, keepdims=True)
        st["h"] = (x * lax.rsqrt(ms + EPS) * ng_ref[...]).astype(BF16)

    def stage_conv(st):
        u = proj(st, C_AC, CONV_CH) * proj(st, C_AX, CONV_CH)
        u_scr[SUBLANES:SUBLANES + ts, :] = u
        u1 = u_scr[SUBLANES - 1:SUBLANES - 1 + ts, :]
        u2 = u_scr[SUBLANES - 2:SUBLANES - 2 + ts, :]
        y = cb_ref[...] + cw_ref[0:1, :] * u2
        y = y + cw_ref[1:2, :] * u1
        y = y + cw_ref[2:3, :] * u
        st["ya"] = (proj(st, C_AB, CONV_CH) * y).astype(BF16)
        u_scr[0:SUBLANES, :] = u_scr[ts:ts + SUBLANES, :]

    def stage_sgu_in(st):
        st["gu"] = _gelu_exact(proj(st, C_SU, SGU_WIDTH))
        gv = _gelu_exact(proj(st, C_SV, SGU_WIDTH))
        mu = jnp.mean(gv, axis=-1, keepdims=True)
        vc = gv - mu
        var = jnp.mean(vc * vc, axis=-1, keepdims=True)
        st["vn"] = (vc * lax.rsqrt(var + EPS) * lng_ref[...] + lnb_ref[...]).astype(BF16)

    def stage_qkv(st):
        rows = st["rows"]
        q = _head_rmsnorm(proj(st, C_Q, ATT_WIDTH), qg_ref[...]) * (HEAD_DIM ** -0.5 * LOG2E)
        q_ref[0, rows, :] = q.astype(BF16)
        k_ref[0, rows, :] = _head_rmsnorm(proj(st, C_K, ATT_WIDTH), kg_ref[...]).astype(BF16)
        v_ref[0, rows, :] = proj(st, C_V, ATT_WIDTH).astype(BF16)

    def stage_sgu_mix(st):
        r0 = st["rows"].start
        for g in range(SGU_GROUPS):
            cs = slice(g * CHUNK, (g + 1) * CHUNK)
            for c in range(ts // CHUNK):
                rs = slice(c * CHUNK, (c + 1) * CHUNK)
                mixed = _dot(sgu_w[g], st["vn"][rs, cs]) + sb_ref[:, cs]
                yb_scr[r0 + c * CHUNK:r0 + (c + 1) * CHUNK, cs] = (
                    st["gu"][rs, cs] * mixed).astype(BF16)

    def stage_gates(st):
        rows = st["rows"]
        ga = _sigmoid(proj(st, C_G, D_MODEL) + bg_ref[:, 0:D_MODEL])
        gb = _sigmoid(proj(st, C_G + D_MODEL, D_MODEL) + bg_ref[:, D_MODEL:2 * D_MODEL])
        mab = ga * _dot(st["ya"], wab_ref[0]) + gb * _dot(yb_scr[rows, :], wab_ref[1])
        mab_ref[0, rows, :] = mab.astype(BF16)
        gc = _sigmoid(proj(st, C_G + 2 * D_MODEL, D_MODEL) + bg_ref[:, 2 * D_MODEL:3 * D_MODEL])
        gc_ref[0, rows, :] = gc.astype(BF16)

    subs = [dict(rows=slice(r0, r0 + ts)) for r0 in range(0, tm, ts)]
    for st in subs:
        stage_norm(st)
    for i in range(len(subs) + 1):
        if i < len(subs):
            stage_conv(subs[i])
            stage_sgu_in(subs[i])
            stage_qkv(subs[i])
        if i >= 1:
            stage_sgu_mix(subs[i - 1])
            stage_gates(subs[i - 1])


def _mixer_in(l, x, ng, win, bg, cw, cb, lng, lnb, sw, sbf, qg, kg, wbo):
    B, S, D = x.shape
    tm = TM_IN
    tok = lambda width: pl.BlockSpec((1, tm, width), lambda b, j: (b, j, 0))
    out_shape = (
        jax.ShapeDtypeStruct((B, S, D), BF16),
        jax.ShapeDtypeStruct((B, S, D), BF16),
        jax.ShapeDtypeStruct((B, S, ATT_WIDTH), BF16),
        jax.ShapeDtypeStruct((B, S, ATT_WIDTH), BF16),
        jax.ShapeDtypeStruct((B, S, ATT_WIDTH), BF16),
    )
    return pl.pallas_call(
        _mixer_in_body,
        grid=(B, S // tm),
        in_specs=[
            tok(D),
            _resident(ng.shape), _resident(win.shape, (l,)), _resident(bg.shape),
            _resident(cw.shape), _resident(cb.shape), _resident(lng.shape),
            _resident(lnb.shape), _resident(sw.shape), _resident(sbf.shape),
            _resident(qg.shape), _resident(kg.shape),
            pl.BlockSpec((None, 2, CONV_CH, D), lambda *_: (l, 0, 0, 0),
                         pipeline_mode=pl.Buffered(1)),
        ],
        out_specs=(tok(D), tok(D), tok(ATT_WIDTH), tok(ATT_WIDTH), tok(ATT_WIDTH)),
        out_shape=out_shape,
        scratch_shapes=[
            pltpu.VMEM((TS_IN + SUBLANES, CONV_CH), F32),
            pltpu.VMEM((tm, SGU_WIDTH), BF16),
        ],
        compiler_params=pltpu.CompilerParams(
            dimension_semantics=("arbitrary", "arbitrary"),
            vmem_limit_bytes=VMEM_LIMIT_BYTES),
        name="mixer_in",
    )(x, ng, win, bg, cw, cb, lng, lnb, sw, sbf, qg, kg, wbo)


def _attention_body(q_ref, k_ref, v_ref, o_ref, acc_scr, carry_scr):
    io = pl.program_id(2)
    n_q = TQ_OUTER // TQ
    lo = lax.broadcasted_iota(jnp.int32, (TQ, LANES), 1) < HEAD_DIM

    def neg_inclusive(n):
        return jnp.where(lax.broadcasted_iota(jnp.int32, (n, n), 0)
                         >= lax.broadcasted_iota(jnp.int32, (n, n), 1), -1.0, 0.0).astype(BF16)

    incl_diag = neg_inclusive(TQ)
    incl_far = neg_inclusive(TK)
    diag_mask = (lax.broadcasted_iota(jnp.int32, (2 * TQ, TQ), 1)
                 < lax.broadcasted_iota(jnp.int32, (2 * TQ, TQ), 0) % TQ)
    far_col = lax.broadcasted_iota(jnp.int32, (2 * TQ, TK), 1)

    def stack_heads(q):
        zero = jnp.zeros_like(q)
        return jnp.concatenate([jnp.where(lo, q, zero), jnp.where(lo, zero, q)], axis=0)

    def logits(q2, kblk, mask):
        z = lax.dot_general(q2, kblk, (((1,), (1,)), ((), ())), preferred_element_type=F32)
        sp = jnp.maximum(z, jnp.log(1.0 + jnp.exp2(jnp.minimum(z, EXP2_CLAMP))) * LOG2E)
        if mask is not None:
            sp = jnp.where(mask, sp, 0.0)
        return z, sp.astype(BF16)

    def weights(z, sp, incl, carry, mask):
        neg_cum = _dot(sp, incl)
        e = z + neg_cum
        if carry is not None:
            e = e + carry
        a = jnp.exp2(e)
        if mask is not None:
            a = jnp.where(mask, a, 0.0)
        return a.astype(BF16), neg_cum[:, 0:1]

    def key_block(q2, kblk, vblk, incl, carry, mask):
        z, sp = logits(q2, kblk, mask)
        a, c = weights(z, sp, incl, carry, mask)
        return _dot(a, vblk), c

    def write_out(q_rows, acc):
        o_ref[0, q_rows, :] = jnp.where(lo, acc[:TQ], acc[TQ:]).astype(o_ref.dtype)

    def stage_logits(qi):
        q_start = pl.multiple_of(io * TQ_OUTER + qi * TQ, TQ)
        far_start = pl.multiple_of(jnp.maximum(q_start - TK, 0), TQ)
        q2 = stack_heads(q_ref[0, qi * TQ:(qi + 1) * TQ, :])
        far_mask = (far_col + far_start) < q_start if qi * TQ < TK else None
        z_d, sp_d = logits(q2, k_ref[0, pl.ds(q_start, TQ), :], diag_mask)
        z_f, sp_f = logits(q2, k_ref[0, pl.ds(far_start, TK), :], far_mask)
        return dict(q_start=q_start, far_start=far_start, far_mask=far_mask,
                    z_d=z_d, sp_d=sp_d, z_f=z_f, sp_f=sp_f)

    def stage_weights(st):
        a_d, c_d = weights(st["z_d"], st["sp_d"], incl_diag, None, diag_mask)
        a_f, c_f = weights(st["z_f"], st["sp_f"], incl_far, c_d, st["far_mask"])
        return dict(q_start=st["q_start"], far_start=st["far_start"],
                    a_d=a_d, a_f=a_f, carry=c_d + c_f)

    def stage_values(st):
        pv_d = _dot(st["a_d"], v_ref[0, pl.ds(st["q_start"], TQ), :])
        pv_f = _dot(st["a_f"], v_ref[0, pl.ds(st["far_start"], TK), :])
        return pv_d + pv_f, st["carry"]

    after_logits, after_weights, accs, carries = {}, {}, [], []
    for t in range(n_q + 2 * STAGE_SKEW):
        if t < n_q:
            after_logits[t] = stage_logits(t)
        if 0 <= t - STAGE_SKEW < n_q:
            after_weights[t - STAGE_SKEW] = stage_weights(after_logits.pop(t - STAGE_SKEW))
        if 0 <= t - 2 * STAGE_SKEW < n_q:
            acc, carry = stage_values(after_weights.pop(t - 2 * STAGE_SKEW))
            accs.append(acc)
            carries.append(carry)

    worst = functools.reduce(jnp.maximum, carries)
    for qi in range(n_q):
        write_out(slice(qi * TQ, (qi + 1) * TQ), accs[qi])
        acc_scr[qi] = accs[qi]
        carry_scr[qi] = jnp.broadcast_to(carries[qi], (2 * TQ, LANES))

    def finish(qi, _):
        q_rows = pl.ds(pl.multiple_of(qi * TQ, TQ), TQ)
        q2 = stack_heads(q_ref[0, q_rows, :])

        def unfinished():
            return jnp.max(carry_scr[qi]) > EXP2_ZERO_BOUND

        def cond(state):
            k_end, go = state
            return jnp.logical_and(k_end > 0, go)

        def body(state):
            k_end, _ = state
            k_start = pl.multiple_of(jnp.maximum(k_end - TK, 0), TQ)
            kblk = k_ref[0, pl.ds(k_start, TK), :]
            vblk = v_ref[0, pl.ds(k_start, TK), :]
            carry = carry_scr[qi]
            pv, c = key_block(q2, kblk, vblk, incl_far, carry[:, 0:1],
                              (far_col + k_start) < k_end)
            acc_scr[qi] += pv
            carry_scr[qi] = carry + c
            return k_end - TK, unfinished()

        lax.while_loop(cond, body, (io * TQ_OUTER + qi * TQ - TK, unfinished()))
        write_out(q_rows, acc_scr[qi])
        return 0

    @pl.when(jnp.max(worst) > EXP2_ZERO_BOUND)
    def _():
        lax.fori_loop(0, n_q, finish, 0)


def _attention(q, k, v):
    B, S, W = q.shape
    n_pairs = W // LANES
    qspec = pl.BlockSpec((1, TQ_OUTER, LANES), lambda b, p, i: (b, i, p))
    kvspec = pl.BlockSpec((1, S, LANES), lambda b, p, i: (b, 0, p))
    return pl.pallas_call(
        _attention_body,
        grid=(B, n_pairs, S // TQ_OUTER),
        in_specs=[qspec, kvspec, kvspec],
        out_specs=qspec,
        out_shape=jax.ShapeDtypeStruct((B, S, W), BF16),
        scratch_shapes=[
            pltpu.VMEM((TQ_OUTER // TQ, 2 * TQ, LANES), F32),
            pltpu.VMEM((TQ_OUTER // TQ, 2 * TQ, LANES), F32),
        ],
        compiler_params=pltpu.CompilerParams(
            dimension_semantics=("arbitrary", "arbitrary", "arbitrary"),
            vmem_limit_bytes=VMEM_LIMIT_BYTES),
        name="attention",
    )(q, k, v)


def _merge_ffn_body(x_ref, mab_ref, gc_ref, yc_ref, wc_ref, wo_ref, fg_ref, wgu_ref, wd_ref,
                    o_ref, act_scr):
    def stage_branch(st):
        st["ydc"] = _dot(yc_ref[st["rows"], :], wc_ref[...])

    def stage_wo(st):
        rows = st["rows"]
        merged = mab_ref[rows, :].astype(F32) + gc_ref[rows, :].astype(F32) * st["ydc"]
        x1 = x_ref[rows, :] + _dot(merged.astype(BF16), wo_ref[...])
        o_ref[rows, :] = x1
        ms = jnp.mean(x1 * x1, axis=-1, keepdims=True)
        st["h2"] = (x1 * lax.rsqrt(ms + EPS) * fg_ref[...]).astype(BF16)

    def stage_gate_up(st, c0):
        c1 = min(c0 + FF_SPLIT, D_FF)
        gate = _dot(st["h2"], wgu_ref[:, c0:c1])
        up = _dot(st["h2"], wgu_ref[:, D_FF + c0:D_FF + c1])
        act_scr[st["rows"], c0:c1] = (gate * _sigmoid(gate) * up).astype(BF16)

    def stage_down(st):
        rows = st["rows"]
        o_ref[rows, :] += _dot(act_scr[rows, :], wd_ref[...])

    subs = [dict(rows=slice(r0, r0 + TS_FFN)) for r0 in range(0, x_ref.shape[0], TS_FFN)]
    for stage in (stage_branch, stage_wo):
        for st in subs:
            stage(st)
    for c0 in range(0, D_FF, FF_SPLIT):
        for st in subs:
            stage_gate_up(st, c0)
    for st in subs:
        stage_down(st)


def _merge_ffn(l, x, mab, gc, yc, wbo, wo, fg, wgu, wd):
    M, D = x.shape
    tm = TM_FFN
    tok = lambda width: pl.BlockSpec((tm, width), lambda i: (i, 0))
    return pl.pallas_call(
        _merge_ffn_body,
        grid=(M // tm,),
        in_specs=[
            tok(D), tok(D), tok(D), tok(ATT_WIDTH),
            _resident(wbo.shape, (l, 2)), _resident(wo.shape, (l,)), _resident(fg.shape),
            _resident(wgu.shape, (l,)), _resident(wd.shape, (l,)),
        ],
        out_specs=tok(D),
        out_shape=jax.ShapeDtypeStruct((M, D), F32),
        scratch_shapes=[pltpu.VMEM((tm, D_FF), BF16)],
        compiler_params=pltpu.CompilerParams(
            dimension_semantics=("arbitrary",),
            vmem_limit_bytes=VMEM_LIMIT_BYTES),
        name="merge_ffn",
    )(x, mab, gc, yc, wbo, wo, fg, wgu, wd)


def kernel(x, mix_norm_g, w_in, b_gate, conv_w, conv_b, sgu_ln_g, sgu_ln_b, sgu_w, sgu_b,
           q_norm_g, k_norm_g, w_branch_out, w_o, ffn_norm_g, w_gate_up, w_down):
    B, S, D = x.shape
    depth = w_in.shape[0]
    row = lambda a: a.reshape(1, -1)
    win = _cast_bf16(w_in)
    wbo = _cast_bf16(w_branch_out.reshape(depth * N_BRANCH, CONV_CH, D)).reshape(w_branch_out.shape)
    wo = _cast_bf16(w_o)
    wgu = _cast_bf16(w_gate_up)
    wd = _cast_bf16(w_down)
    for l in range(depth):
        sbf = jnp.repeat(sgu_b[l].T, CHUNK, axis=1)
        qg = jnp.tile(q_norm_g[l], LANES // HEAD_DIM).reshape(1, LANES)
        kg = jnp.tile(k_norm_g[l], LANES // HEAD_DIM).reshape(1, LANES)
        mab, gc, q, k, v = _mixer_in(
            l, x, row(mix_norm_g[l]), win, row(b_gate[l]),
            conv_w[l], row(conv_b[l]), row(sgu_ln_g[l]), row(sgu_ln_b[l]),
            sgu_w[l], sbf, qg, kg, wbo)
        yc = _attention(q, k, v)
        x = _merge_ffn(
            l, x.reshape(B * S, D), mab.reshape(B * S, D), gc.reshape(B * S, D),
            yc.reshape(B * S, ATT_WIDTH), wbo, wo, row(ffn_norm_g[l]), wgu, wd).reshape(B, S, D)
    return x
```

```python
import functools

import jax
import jax.numpy as jnp
from jax import lax
from jax.experimental import pallas as pl
from jax.experimental.pallas import tpu as pltpu

F32 = jnp.float32
BF16 = jnp.bfloat16

D_MODEL = 1024
CONV_CH = 512
CONV_K = 3
SGU_WIDTH = 512
SGU_GROUPS = 4
CHUNK = 128
ATT_HEADS = 8
HEAD_DIM = 64
ATT_WIDTH = ATT_HEADS * HEAD_DIM
N_BRANCH = 3
D_FF = 2816
IN_COLS = 7168
EPS = 1e-6

LANES = 128
SUBLANES = 8
VMEM_LIMIT_BYTES = 56 * 1024 * 1024

C_AB, C_AC, C_AX = 0, 512, 1024
C_SU, C_SV = 1536, 2048
C_Q, C_K, C_V = 2560, 3072, 3584
C_G = 4096

TM_IN = 1024
TS_IN = 512
TM_FFN = 1024
TS_FFN = 512
TQ_OUTER = 4096
TQ = 128
TK = 256
MXU_DIM = 256
FF_SPLIT = 6 * MXU_DIM
CAST_BLOCK_BYTES = 8 * 1024 * 1024

LOG2E = 1.4426950408889634
EXP2_CLAMP = 126.0
STAGE_SKEW = 2

EXP2_ZERO_BOUND = -151.0


def _sigmoid(x):
    return 1.0 / (1.0 + jnp.exp(-x))


def _gelu_exact(x):
    return 0.5 * x * (1.0 + lax.erf(x * (2.0 ** -0.5)))


def _dot(a, b):
    return jnp.dot(a, b, preferred_element_type=F32)


def _resident(shape, lead=()):
    block = (None,) * len(lead) + tuple(shape[len(lead):])
    index = tuple(lead) + (0,) * (len(shape) - len(lead))
    return pl.BlockSpec(block, lambda *_: index, pipeline_mode=pl.Buffered(1))


def _cast_body(w_ref, o_ref):
    o_ref[...] = w_ref[...].astype(o_ref.dtype)


def _cast_rows(n_rows, n_cols):
    best = None
    for rows in range(2 * SUBLANES, n_rows + 1, 2 * SUBLANES):
        if n_rows % rows == 0 and rows * n_cols * 4 <= CAST_BLOCK_BYTES:
            best = rows
    assert best is not None, (n_rows, n_cols)
    return best


def _cast_bf16(w):
    L, R, C = w.shape
    rows = _cast_rows(R, C)
    spec = pl.BlockSpec((1, rows, C), lambda l, r: (l, r, 0))
    return pl.pallas_call(
        _cast_body,
        grid=(L, R // rows),
        in_specs=[spec],
        out_specs=spec,
        out_shape=jax.ShapeDtypeStruct(w.shape, BF16),
        compiler_params=pltpu.CompilerParams(
            dimension_semantics=("arbitrary", "arbitrary"),
            vmem_limit_bytes=VMEM_LIMIT_BYTES),
        name="cast_bf16",
    )(w)


def _head_rmsnorm(p, g128):
    tm = p.shape[0]
    lo = lax.broadcasted_iota(jnp.int32, (tm, LANES), 1) < HEAD_DIM
    outs = []
    for c in range(ATT_WIDTH // LANES):
        pc = p[:, c * LANES:(c + 1) * LANES]
        pp = pc * pc
        s_lo = jnp.sum(jnp.where(lo, pp, 0.0), axis=-1, keepdims=True)
        s_hi = jnp.sum(jnp.where(lo, 0.0, pp), axis=-1, keepdims=True)
        ms = jnp.where(lo, s_lo, s_hi) * (1.0 / HEAD_DIM)
        outs.append(pc * lax.rsqrt(ms + EPS) * g128)
    return jnp.concatenate(outs, axis=-1)


def _mixer_in_body(x_ref, ng_ref, win_ref, bg_ref, cw_ref, cb_ref, lng_ref, lnb_ref,
                   sw_ref, sb_ref, qg_ref, kg_ref, wab_ref,
                   mab_ref, gc_ref, q_ref, k_ref, v_ref,
                   u_scr, yb_scr):
    tm = x_ref.shape[1]
    ts = TS_IN
    j = pl.program_id(1)

    @pl.when(j == 0)
    def _():
        u_scr[0:SUBLANES, :] = jnp.zeros((SUBLANES, CONV_CH), F32)

    row = lax.broadcasted_iota(jnp.int32, (CHUNK, CHUNK), 0)
    col = lax.broadcasted_iota(jnp.int32, (CHUNK, CHUNK), 1)
    sgu_w = [jnp.where(col <= row, sw_ref[g], 0.0).astype(BF16) for g in range(SGU_GROUPS)]

    def proj(st, c0, width):
        return _dot(st["h"], win_ref[:, c0:c0 + width])

    def stage_norm(st):
        x = x_ref[0, st["rows"], :]
        ms = jnp.mean(x * x, axis=-1, keepdims=True)
        st["h"] = (x * lax.rsqrt(ms + EPS) * ng_ref[...]).astype(BF16)

    def stage_conv(st):
        u = proj(st, C_AC, CONV_CH) * proj(st, C_AX, CONV_CH)
        u_scr[SUBLANES:SUBLANES + ts, :] = u
        u1 = u_scr[SUBLANES - 1:SUBLANES - 1 + ts, :]
        u2 = u_scr[SUBLANES - 2:SUBLANES - 2 + ts, :]
        y = cb_ref[...] + cw_ref[0:1, :] * u2
        y = y + cw_ref[1:2, :] * u1
        y = y + cw_ref[2:3, :] * u
        st["ya"] = (proj(st, C_AB, CONV_CH) * y).astype(BF16)
        u_scr[0:SUBLANES, :] = u_scr[ts:ts + SUBLANES, :]

    def stage_sgu_in(st):
        st["gu"] = _gelu_exact(proj(st, C_SU, SGU_WIDTH))
        gv = _gelu_exact(proj(st, C_SV, SGU_WIDTH))
        mu = jnp.mean(gv, axis=-1, keepdims=True)
        vc = gv - mu
        var = jnp.mean(vc * vc, axis=-1, keepdims=True)
        st["vn"] = (vc * lax.rsqrt(var + EPS) * lng_ref[...] + lnb_ref[...]).astype(BF16)

    def stage_qkv(st):
        rows = st["rows"]
        q = _head_rmsnorm(proj(st, C_Q, ATT_WIDTH), qg_ref[...]) * (HEAD_DIM ** -0.5 * LOG2E)
        q_ref[0, rows, :] = q.astype(BF16)
        k_ref[0, rows, :] = _head_rmsnorm(proj(st, C_K, ATT_WIDTH), kg_ref[...]).astype(BF16)
        v_ref[0, rows, :] = proj(st, C_V, ATT_WIDTH).astype(BF16)

    def stage_sgu_mix(st):
        r0 = st["rows"].start
        for g in range(SGU_GROUPS):
            cs = slice(g * CHUNK, (g + 1) * CHUNK)
            for c in range(ts // CHUNK):
                rs = slice(c * CHUNK, (c + 1) * CHUNK)
                mixed = _dot(sgu_w[g], st["vn"][rs, cs]) + sb_ref[:, cs]
                yb_scr[r0 + c * CHUNK:r0 + (c + 1) * CHUNK, cs] = (
                    st["gu"][rs, cs] * mixed).astype(BF16)

    def stage_gates(st):
        rows = st["rows"]
        ga = _sigmoid(proj(st, C_G, D_MODEL) + bg_ref[:, 0:D_MODEL])
        gb = _sigmoid(proj(st, C_G + D_MODEL, D_MODEL) + bg_ref[:, D_MODEL:2 * D_MODEL])
        mab = ga * _dot(st["ya"], wab_ref[0]) + gb * _dot(yb_scr[rows, :], wab_ref[1])
        mab_ref[0, rows, :] = mab.astype(BF16)
        gc = _sigmoid(proj(st, C_G + 2 * D_MODEL, D_MODEL) + bg_ref[:, 2 * D_MODEL:3 * D_MODEL])
        gc_ref[0, rows, :] = gc.astype(BF16)

    subs = [dict(rows=slice(r0, r0 + ts)) for r0 in range(0, tm, ts)]
    for st in subs:
        stage_norm(st)
    for i in range(len(subs) + 1):
        if i < len(subs):
            stage_conv(subs[i])
            stage_sgu_in(subs[i])
            stage_qkv(subs[i])
        if i >= 1:
            stage_sgu_mix(subs[i - 1])
            stage_gates(subs[i - 1])


def _mixer_in(l, x, ng, win, bg, cw, cb, lng, lnb, sw, sbf, qg, kg, wbo):
    B, S, D = x.shape
    tm = TM_IN
    tok = lambda width: pl.BlockSpec((1, tm, width), lambda b, j: (b, j, 0))
    out_shape = (
        jax.ShapeDtypeStruct((B, S, D), BF16),
        jax.ShapeDtypeStruct((B, S, D), BF16),
        jax.ShapeDtypeStruct((B, S, ATT_WIDTH), BF16),
        jax.ShapeDtypeStruct((B, S, ATT_WIDTH), BF16),
        jax.ShapeDtypeStruct((B, S, ATT_WIDTH), BF16),
    )
    return pl.pallas_call(
        _mixer_in_body,
        grid=(B, S // tm),
        in_specs=[
            tok(D),
            _resident(ng.shape), _resident(win.shape, (l,)), _resident(bg.shape),
            _resident(cw.shape), _resident(cb.shape), _resident(lng.shape),
            _resident(lnb.shape), _resident(sw.shape), _resident(sbf.shape),
            _resident(qg.shape), _resident(kg.shape),
            pl.BlockSpec((None, 2, CONV_CH, D), lambda *_: (l, 0, 0, 0),
                         pipeline_mode=pl.Buffered(1)),
        ],
        out_specs=(tok(D), tok(D), tok(ATT_WIDTH), tok(ATT_WIDTH), tok(ATT_WIDTH)),
        out_shape=out_shape,
        scratch_shapes=[
            pltpu.VMEM((TS_IN + SUBLANES, CONV_CH), F32),
            pltpu.VMEM((tm, SGU_WIDTH), BF16),
        ],
        compiler_params=pltpu.CompilerParams(
            dimension_semantics=("arbitrary", "arbitrary"),
            vmem_limit_bytes=VMEM_LIMIT_BYTES),
        name="mixer_in",
    )(x, ng, win, bg, cw, cb, lng, lnb, sw, sbf, qg, kg, wbo)


def _attention_body(q_ref, k_ref, v_ref, o_ref, acc_scr, carry_scr):
    io = pl.program_id(2)
    n_q = TQ_OUTER // TQ
    lo = lax.broadcasted_iota(jnp.int32, (TQ, LANES), 1) < HEAD_DIM

    def neg_inclusive(n):
        return jnp.where(lax.broadcasted_iota(jnp.int32, (n, n), 0)
                         >= lax.broadcasted_iota(jnp.int32, (n, n), 1), -1.0, 0.0).astype(BF16)

    incl_diag = neg_inclusive(TQ)
    incl_far = neg_inclusive(TK)
    diag_mask = (lax.broadcasted_iota(jnp.int32, (2 * TQ, TQ), 1)
                 < lax.broadcasted_iota(jnp.int32, (2 * TQ, TQ), 0) % TQ)
    far_col = lax.broadcasted_iota(jnp.int32, (2 * TQ, TK), 1)

    def stack_heads(q):
        zero = jnp.zeros_like(q)
        return jnp.concatenate([jnp.where(lo, q, zero), jnp.where(lo, zero, q)], axis=0)

    def logits(q2, kblk, mask):
        z = lax.dot_general(q2, kblk, (((1,), (1,)), ((), ())), preferred_element_type=F32)
        sp = jnp.maximum(z, jnp.log(1.0 + jnp.exp2(jnp.minimum(z, EXP2_CLAMP))) * LOG2E)
        if mask is not None:
            sp = jnp.where(mask, sp, 0.0)
        return z, sp.astype(BF16)

    def weights(z, sp, incl, carry, mask):
        neg_cum = _dot(sp, incl)
        e = z + neg_cum
        if carry is not None:
            e = e + carry
        a = jnp.exp2(e)
        if mask is not None:
            a = jnp.where(mask, a, 0.0)
        return a.astype(BF16), neg_cum[:, 0:1]

    def key_block(q2, kblk, vblk, incl, carry, mask):
        z, sp = logits(q2, kblk, mask)
        a, c = weights(z, sp, incl, carry, mask)
        return _dot(a, vblk), c

    def write_out(q_rows, acc):
        o_ref[0, q_rows, :] = jnp.where(lo, acc[:TQ], acc[TQ:]).astype(o_ref.dtype)

    def stage_logits(qi):
        q_start = pl.multiple_of(io * TQ_OUTER + qi * TQ, TQ)
        far_start = pl.multiple_of(jnp.maximum(q_start - TK, 0), TQ)
        q2 = stack_heads(q_ref[0, qi * TQ:(qi + 1) * TQ, :])
        far_mask = (far_col + far_start) < q_start if qi * TQ < TK else None
        z_d, sp_d = logits(q2, k_ref[0, pl.ds(q_start, TQ), :], diag_mask)
        z_f, sp_f = logits(q2, k_ref[0, pl.ds(far_start, TK), :], far_mask)
        return dict(q_start=q_start, far_start=far_start, far_mask=far_mask,
                    z_d=z_d, sp_d=sp_d, z_f=z_f, sp_f=sp_f)

    def stage_weights(st):
        a_d, c_d = weights(st["z_d"], st["sp_d"], incl_diag, None, diag_mask)
        a_f, c_f = weights(st["z_f"], st["sp_f"], incl_far, c_d, st["far_mask"])
        return dict(q_start=st["q_start"], far_start=st["far_start"],
                    a_d=a_d, a_f=a_f, carry=c_d + c_f)

    def stage_values(st):
        pv_d = _dot(st["a_d"], v_ref[0, pl.ds(st["q_start"], TQ), :])
        pv_f = _dot(st["a_f"], v_ref[0, pl.ds(st["far_start"], TK), :])
        return pv_d + pv_f, st["carry"]

    after_logits, after_weights, accs, carries = {}, {}, [], []
    for t in range(n_q + 2 * STAGE_SKEW):
        if t < n_q:
            after_logits[t] = stage_logits(t)
        if 0 <= t - STAGE_SKEW < n_q:
            after_weights[t - STAGE_SKEW] = stage_weights(after_logits.pop(t - STAGE_SKEW))
        if 0 <= t - 2 * STAGE_SKEW < n_q:
            acc, carry = stage_values(after_weights.pop(t - 2 * STAGE_SKEW))
            accs.append(acc)
            carries.append(carry)

    n_first = TK // TQ + 1
    worst = jnp.max(functools.reduce(jnp.maximum, carries[n_first:]))
    worst_first = jnp.max(functools.reduce(jnp.maximum, carries[:n_first]))
    worst = jnp.where(io > 0, jnp.maximum(worst, worst_first), worst)
    for qi in range(n_q):
        write_out(slice(qi * TQ, (qi + 1) * TQ), accs[qi])
        acc_scr[qi] = accs[qi]
        carry_scr[qi] = jnp.broadcast_to(carries[qi], (2 * TQ, LANES))

    def finish(qi, _):
        q_rows = pl.ds(pl.multiple_of(qi * TQ, TQ), TQ)
        q2 = stack_heads(q_ref[0, q_rows, :])

        def unfinished():
            return jnp.max(carry_scr[qi]) > EXP2_ZERO_BOUND

        def cond(state):
            k_end, go = state
            return jnp.logical_and(k_end > 0, go)

        def body(state):
            k_end, _ = state
            k_start = pl.multiple_of(jnp.maximum(k_end - TK, 0), TQ)
            kblk = k_ref[0, pl.ds(k_start, TK), :]
            vblk = v_ref[0, pl.ds(k_start, TK), :]
            carry = carry_scr[qi]
            pv, c = key_block(q2, kblk, vblk, incl_far, carry[:, 0:1],
                              (far_col + k_start) < k_end)
            acc_scr[qi] += pv
            carry_scr[qi] = carry + c
            return k_end - TK, unfinished()

        lax.while_loop(cond, body, (io * TQ_OUTER + qi * TQ - TK, unfinished()))
        write_out(q_rows, acc_scr[qi])
        return 0

    @pl.when(worst > EXP2_ZERO_BOUND)
    def _():
        lax.fori_loop(0, n_q, finish, 0)


def _attention(q, k, v):
    B, S, W = q.shape
    n_pairs = W // LANES
    qspec = pl.BlockSpec((1, TQ_OUTER, LANES), lambda b, p, i: (b, i, p))
    kvspec = pl.BlockSpec((1, S, LANES), lambda b, p, i: (b, 0, p))
    return pl.pallas_call(
        _attention_body,
        grid=(B, n_pairs, S // TQ_OUTER),
        in_specs=[qspec, kvspec, kvspec],
        out_specs=qspec,
        out_shape=jax.ShapeDtypeStruct((B, S, W), BF16),
        scratch_shapes=[
            pltpu.VMEM((TQ_OUTER // TQ, 2 * TQ, LANES), F32),
            pltpu.VMEM((TQ_OUTER // TQ, 2 * TQ, LANES), F32),
        ],
        compiler_params=pltpu.CompilerParams(
            dimension_semantics=("arbitrary", "arbitrary", "arbitrary"),
            vmem_limit_bytes=VMEM_LIMIT_BYTES),
        name="attention",
    )(q, k, v)


def _merge_ffn_body(x_ref, mab_ref, gc_ref, yc_ref, wc_ref, wo_ref, fg_ref, wgu_ref, wd_ref,
                    o_ref, act_scr):
    def stage_branch(st):
        st["ydc"] = _dot(yc_ref[st["rows"], :], wc_ref[...])

    def stage_wo(st):
        rows = st["rows"]
        merged = mab_ref[rows, :].astype(F32) + gc_ref[rows, :].astype(F32) * st["ydc"]
        x1 = x_ref[rows, :] + _dot(merged.astype(BF16), wo_ref[...])
        o_ref[rows, :] = x1
        ms = jnp.mean(x1 * x1, axis=-1, keepdims=True)
        st["h2"] = (x1 * lax.rsqrt(ms + EPS) * fg_ref[...]).astype(BF16)

    def stage_gate_up(st, c0):
        c1 = min(c0 + FF_SPLIT, D_FF)
        gate = _dot(st["h2"], wgu_ref[:, c0:c1])
        up = _dot(st["h2"], wgu_ref[:, D_FF + c0:D_FF + c1])
        act_scr[st["rows"], c0:c1] = (gate * _sigmoid(gate) * up).astype(BF16)

    def stage_down(st):
        rows = st["rows"]
        o_ref[rows, :] += _dot(act_scr[rows, :], wd_ref[...])

    subs = [dict(rows=slice(r0, r0 + TS_FFN)) for r0 in range(0, x_ref.shape[0], TS_FFN)]
    for stage in (stage_branch, stage_wo):
        for st in subs:
            stage(st)
    for c0 in range(0, D_FF, FF_SPLIT):
        for st in subs:
            stage_gate_up(st, c0)
    for st in subs:
        stage_down(st)


def _merge_ffn(l, x, mab, gc, yc, wbo, wo, fg, wgu, wd):
    M, D = x.shape
    tm = TM_FFN
    tok = lambda width: pl.BlockSpec((tm, width), lambda i: (i, 0))
    return pl.pallas_call(
        _merge_ffn_body,
        grid=(M // tm,),
        in_specs=[
            tok(D), tok(D), tok(D), tok(ATT_WIDTH),
            _resident(wbo.shape, (l, 2)), _resident(wo.shape, (l,)), _resident(fg.shape),
            _resident(wgu.shape, (l,)), _resident(wd.shape, (l,)),
        ],
        out_specs=tok(D),
        out_shape=jax.ShapeDtypeStruct((M, D), F32),
        scratch_shapes=[pltpu.VMEM((tm, D_FF), BF16)],
        compiler_params=pltpu.CompilerParams(
            dimension_semantics=("arbitrary",),
            vmem_limit_bytes=VMEM_LIMIT_BYTES),
        name="merge_ffn",
    )(x, mab, gc, yc, wbo, wo, fg, wgu, wd)


def kernel(x, mix_norm_g, w_in, b_gate, conv_w, conv_b, sgu_ln_g, sgu_ln_b, sgu_w, sgu_b,
           q_norm_g, k_norm_g, w_branch_out, w_o, ffn_norm_g, w_gate_up, w_down):
    B, S, D = x.shape
    depth = w_in.shape[0]
    row = lambda a: a.reshape(1, -1)
    win = _cast_bf16(w_in)
    wbo = _cast_bf16(w_branch_out.reshape(depth * N_BRANCH, CONV_CH, D)).reshape(w_branch_out.shape)
    wo = _cast_bf16(w_o)
    wgu = _cast_bf16(w_gate_up)
    wd = _cast_bf16(w_down)
    for l in range(depth):
        sbf = jnp.repeat(sgu_b[l].T, CHUNK, axis=1)
        qg = jnp.tile(q_norm_g[l], LANES // HEAD_DIM).reshape(1, LANES)
        kg = jnp.tile(k_norm_g[l], LANES // HEAD_DIM).reshape(1, LANES)
        mab, gc, q, k, v = _mixer_in(
            l, x, row(mix_norm_g[l]), win, row(b_gate[l]),
            conv_w[l], row(conv_b[l]), row(sgu_ln_g[l]), row(sgu_ln_b[l]),
            sgu_w[l], sbf, qg, kg, wbo)
        yc = _attention(q, k, v)
        x = _merge_ffn(
            l, x.reshape(B * S, D), mab.reshape(B * S, D), gc.reshape(B * S, D),
            yc.reshape(B * S, ATT_WIDTH), wbo, wo, row(ffn_norm_g[l]), wgu, wd).reshape(B, S, D)
    return x
```

```python
import functools

import jax
import jax.numpy as jnp
from jax import lax
from jax.experimental import pallas as pl
from jax.experimental.pallas import tpu as pltpu

F32 = jnp.float32
BF16 = jnp.bfloat16

D_MODEL = 1024
CONV_CH = 512
CONV_K = 3
SGU_WIDTH = 512
SGU_GROUPS = 4
CHUNK = 128
ATT_HEADS = 8
HEAD_DIM = 64
ATT_WIDTH = ATT_HEADS * HEAD_DIM
N_BRANCH = 3
D_FF = 2816
IN_COLS = 7168
EPS = 1e-6

LANES = 128
SUBLANES = 8
VMEM_LIMIT_BYTES = 56 * 1024 * 1024

C_AB, C_AC, C_AX = 0, 512, 1024
C_SU, C_SV = 1536, 2048
C_Q, C_K, C_V = 2560, 3072, 3584
C_G = 4096

TM_IN = 1024
TS_IN = 512
TM_FFN = 1024
TS_FFN = 512
TQ_OUTER = 4096
TQ = 128
TK = 256
MXU_DIM = 256
FF_SPLIT = 6 * MXU_DIM
CAST_BLOCK_BYTES = 8 * 1024 * 1024

LOG2E = 1.4426950408889634
EXP2_CLAMP = 126.0
STAGE_SKEW = 2

EXP2_ZERO_BOUND = -151.0


def _sigmoid(x):
    return 1.0 / (1.0 + jnp.exp(-x))


def _gelu_exact(x):
    return 0.5 * x * (1.0 + lax.erf(x * (2.0 ** -0.5)))


def _dot(a, b):
    return jnp.dot(a, b, preferred_element_type=F32)


def _resident(shape):
    zeros = (0,) * len(shape)
    return pl.BlockSpec(shape, lambda *_: zeros, pipeline_mode=pl.Buffered(1))


def _cast_body(w_ref, o_ref):
    o_ref[...] = w_ref[...].astype(o_ref.dtype)


def _cast_rows(n_rows, n_cols):
    best = None
    for rows in range(2 * SUBLANES, n_rows + 1, 2 * SUBLANES):
        if n_rows % rows == 0 and rows * n_cols * 4 <= CAST_BLOCK_BYTES:
            best = rows
    assert best is not None, (n_rows, n_cols)
    return best


def _cast_bf16(w, layer):
    _, R, C = w.shape
    rows = _cast_rows(R, C)
    return pl.pallas_call(
        _cast_body,
        grid=(R // rows,),
        in_specs=[pl.BlockSpec((None, rows, C), lambda r: (layer, r, 0))],
        out_specs=pl.BlockSpec((rows, C), lambda r: (r, 0)),
        out_shape=jax.ShapeDtypeStruct((R, C), BF16),
        compiler_params=pltpu.CompilerParams(
            dimension_semantics=("arbitrary",),
            vmem_limit_bytes=VMEM_LIMIT_BYTES),
        name="cast_bf16",
    )(w)


def _head_rmsnorm(p, g128):
    tm = p.shape[0]
    lo = lax.broadcasted_iota(jnp.int32, (tm, LANES), 1) < HEAD_DIM
    outs = []
    for c in range(ATT_WIDTH // LANES):
        pc = p[:, c * LANES:(c + 1) * LANES]
        pp = pc * pc
        s_lo = jnp.sum(jnp.where(lo, pp, 0.0), axis=-1, keepdims=True)
        s_hi = jnp.sum(jnp.where(lo, 0.0, pp), axis=-1, keepdims=True)
        ms = jnp.where(lo, s_lo, s_hi) * (1.0 / HEAD_DIM)
        outs.append(pc * lax.rsqrt(ms + EPS) * g128)
    return jnp.concatenate(outs, axis=-1)


def _mixer_in_body(x_ref, ng_ref, win_ref, bg_ref, cw_ref, cb_ref, lng_ref, lnb_ref,
                   sw_ref, sb_ref, qg_ref, kg_ref, wab_ref,
                   mab_ref, gc_ref, q_ref, k_ref, v_ref,
                   u_scr, yb_scr):
    tm = x_ref.shape[1]
    ts = TS_IN
    j = pl.program_id(1)

    @pl.when(j == 0)
    def _():
        u_scr[0:SUBLANES, :] = jnp.zeros((SUBLANES, CONV_CH), F32)

    row = lax.broadcasted_iota(jnp.int32, (CHUNK, CHUNK), 0)
    col = lax.broadcasted_iota(jnp.int32, (CHUNK, CHUNK), 1)
    sgu_w = [jnp.where(col <= row, sw_ref[g], 0.0).astype(BF16) for g in range(SGU_GROUPS)]

    def proj(st, c0, width):
        return _dot(st["h"], win_ref[:, c0:c0 + width])

    def stage_norm(st):
        x = x_ref[0, st["rows"], :]
        ms = jnp.mean(x * x, axis=-1, keepdims=True)
        st["h"] = (x * lax.rsqrt(ms + EPS) * ng_ref[...]).astype(BF16)

    def stage_conv(st):
        u = proj(st, C_AC, CONV_CH) * proj(st, C_AX, CONV_CH)
        u_scr[SUBLANES:SUBLANES + ts, :] = u
        u1 = u_scr[SUBLANES - 1:SUBLANES - 1 + ts, :]
        u2 = u_scr[SUBLANES - 2:SUBLANES - 2 + ts, :]
        y = cb_ref[...] + cw_ref[0:1, :] * u2
        y = y + cw_ref[1:2, :] * u1
        y = y + cw_ref[2:3, :] * u
        st["ya"] = (proj(st, C_AB, CONV_CH) * y).astype(BF16)
        u_scr[0:SUBLANES, :] = u_scr[ts:ts + SUBLANES, :]

    def stage_sgu_in(st):
        st["gu"] = _gelu_exact(proj(st, C_SU, SGU_WIDTH))
        gv = _gelu_exact(proj(st, C_SV, SGU_WIDTH))
        mu = jnp.mean(gv, axis=-1, keepdims=True)
        vc = gv - mu
        var = jnp.mean(vc * vc, axis=-1, keepdims=True)
        st["vn"] = (vc * lax.rsqrt(var + EPS) * lng_ref[...] + lnb_ref[...]).astype(BF16)

    def stage_qkv(st):
        rows = st["rows"]
        q = _head_rmsnorm(proj(st, C_Q, ATT_WIDTH), qg_ref[...]) * (HEAD_DIM ** -0.5 * LOG2E)
        q_ref[0, rows, :] = q.astype(BF16)
        k_ref[0, rows, :] = _head_rmsnorm(proj(st, C_K, ATT_WIDTH), kg_ref[...]).astype(BF16)
        v_ref[0, rows, :] = proj(st, C_V, ATT_WIDTH).astype(BF16)

    def stage_sgu_mix(st):
        r0 = st["rows"].start
        for g in range(SGU_GROUPS):
            cs = slice(g * CHUNK, (g + 1) * CHUNK)
            for c in range(ts // CHUNK):
                rs = slice(c * CHUNK, (c + 1) * CHUNK)
                mixed = _dot(sgu_w[g], st["vn"][rs, cs]) + sb_ref[:, cs]
                yb_scr[r0 + c * CHUNK:r0 + (c + 1) * CHUNK, cs] = (
                    st["gu"][rs, cs] * mixed).astype(BF16)

    def stage_gates(st):
        rows = st["rows"]
        ga = _sigmoid(proj(st, C_G, D_MODEL) + bg_ref[:, 0:D_MODEL])
        gb = _sigmoid(proj(st, C_G + D_MODEL, D_MODEL) + bg_ref[:, D_MODEL:2 * D_MODEL])
        mab = ga * _dot(st["ya"], wab_ref[0]) + gb * _dot(yb_scr[rows, :], wab_ref[1])
        mab_ref[0, rows, :] = mab.astype(BF16)
        gc = _sigmoid(proj(st, C_G + 2 * D_MODEL, D_MODEL) + bg_ref[:, 2 * D_MODEL:3 * D_MODEL])
        gc_ref[0, rows, :] = gc.astype(BF16)

    subs = [dict(rows=slice(r0, r0 + ts)) for r0 in range(0, tm, ts)]
    for st in subs:
        stage_norm(st)
    for i in range(len(subs) + 1):
        if i < len(subs):
            stage_conv(subs[i])
            stage_sgu_in(subs[i])
            stage_qkv(subs[i])
        if i >= 1:
            stage_sgu_mix(subs[i - 1])
            stage_gates(subs[i - 1])


def _mixer_in(x, ng, win, bg, cw, cb, lng, lnb, sw, sbf, qg, kg, wbo):
    B, S, D = x.shape
    tm = TM_IN
    tok = lambda width: pl.BlockSpec((1, tm, width), lambda b, j: (b, j, 0))
    out_shape = (
        jax.ShapeDtypeStruct((B, S, D), BF16),
        jax.ShapeDtypeStruct((B, S, D), BF16),
        jax.ShapeDtypeStruct((B, S, ATT_WIDTH), BF16),
        jax.ShapeDtypeStruct((B, S, ATT_WIDTH), BF16),
        jax.ShapeDtypeStruct((B, S, ATT_WIDTH), BF16),
    )
    return pl.pallas_call(
        _mixer_in_body,
        grid=(B, S // tm),
        in_specs=[
            tok(D),
            _resident(ng.shape), _resident(win.shape), _resident(bg.shape),
            _resident(cw.shape), _resident(cb.shape), _resident(lng.shape),
            _resident(lnb.shape), _resident(sw.shape), _resident(sbf.shape),
            _resident(qg.shape), _resident(kg.shape),
            _resident((2, CONV_CH, D)),
        ],
        out_specs=(tok(D), tok(D), tok(ATT_WIDTH), tok(ATT_WIDTH), tok(ATT_WIDTH)),
        out_shape=out_shape,
        scratch_shapes=[
            pltpu.VMEM((TS_IN + SUBLANES, CONV_CH), F32),
            pltpu.VMEM((tm, SGU_WIDTH), BF16),
        ],
        compiler_params=pltpu.CompilerParams(
            dimension_semantics=("arbitrary", "arbitrary"),
            vmem_limit_bytes=VMEM_LIMIT_BYTES),
        name="mixer_in",
    )(x, ng, win, bg, cw, cb, lng, lnb, sw, sbf, qg, kg, wbo)


def _attention_body(n_cast, q_ref, k_ref, v_ref, *refs):
    w_refs, o_ref = refs[:n_cast], refs[n_cast]
    wo_refs = refs[n_cast + 1:2 * n_cast + 1]
    acc_scr, carry_scr = refs[2 * n_cast + 1:]
    for w_ref, wo_ref in zip(w_refs, wo_refs):
        wo_ref[...] = w_ref[...].astype(wo_ref.dtype)

    io = pl.program_id(2)
    n_q = TQ_OUTER // TQ
    lo = lax.broadcasted_iota(jnp.int32, (TQ, LANES), 1) < HEAD_DIM

    def neg_inclusive(n):
        return jnp.where(lax.broadcasted_iota(jnp.int32, (n, n), 0)
                         >= lax.broadcasted_iota(jnp.int32, (n, n), 1), -1.0, 0.0).astype(BF16)

    incl_diag = neg_inclusive(TQ)
    incl_far = neg_inclusive(TK)
    diag_mask = (lax.broadcasted_iota(jnp.int32, (2 * TQ, TQ), 1)
                 < lax.broadcasted_iota(jnp.int32, (2 * TQ, TQ), 0) % TQ)
    far_col = lax.broadcasted_iota(jnp.int32, (2 * TQ, TK), 1)

    def stack_heads(q):
        zero = jnp.zeros_like(q)
        return jnp.concatenate([jnp.where(lo, q, zero), jnp.where(lo, zero, q)], axis=0)

    def logits(q2, kblk, mask):
        z = lax.dot_general(q2, kblk, (((1,), (1,)), ((), ())), preferred_element_type=F32)
        sp = jnp.maximum(z, jnp.log(1.0 + jnp.exp2(jnp.minimum(z, EXP2_CLAMP))) * LOG2E)
        if mask is not None:
            sp = jnp.where(mask, sp, 0.0)
        return z, sp.astype(BF16)

    def weights(z, sp, incl, carry, mask):
        neg_cum = _dot(sp, incl)
        e = z + neg_cum
        if carry is not None:
            e = e + carry
        a = jnp.exp2(e)
        if mask is not None:
            a = jnp.where(mask, a, 0.0)
        return a.astype(BF16), neg_cum[:, 0:1]

    def key_block(q2, kblk, vblk, incl, carry, mask):
        z, sp = logits(q2, kblk, mask)
        a, c = weights(z, sp, incl, carry, mask)
        return _dot(a, vblk), c

    def write_out(q_rows, acc):
        o_ref[0, q_rows, :] = jnp.where(lo, acc[:TQ], acc[TQ:]).astype(o_ref.dtype)

    def stage_logits(qi):
        q_start = pl.multiple_of(io * TQ_OUTER + qi * TQ, TQ)
        far_start = pl.multiple_of(jnp.maximum(q_start - TK, 0), TQ)
        q2 = stack_heads(q_ref[0, qi * TQ:(qi + 1) * TQ, :])
        far_mask = (far_col + far_start) < q_start if qi * TQ < TK else None
        z_d, sp_d = logits(q2, k_ref[0, pl.ds(q_start, TQ), :], diag_mask)
        z_f, sp_f = logits(q2, k_ref[0, pl.ds(far_start, TK), :], far_mask)
        return dict(q_start=q_start, far_start=far_start, far_mask=far_mask,
                    z_d=z_d, sp_d=sp_d, z_f=z_f, sp_f=sp_f)

    def stage_weights(st):
        a_d, c_d = weights(st["z_d"], st["sp_d"], incl_diag, None, diag_mask)
        a_f, c_f = weights(st["z_f"], st["sp_f"], incl_far, c_d, st["far_mask"])
        return dict(q_start=st["q_start"], far_start=st["far_start"],
                    a_d=a_d, a_f=a_f, carry=c_d + c_f)

    def stage_values(st):
        pv_d = _dot(st["a_d"], v_ref[0, pl.ds(st["q_start"], TQ), :])
        pv_f = _dot(st["a_f"], v_ref[0, pl.ds(st["far_start"], TK), :])
        return pv_d + pv_f, st["carry"]

    after_logits, after_weights, accs, carries = {}, {}, [], []
    for t in range(n_q + 2 * STAGE_SKEW):
        if t < n_q:
            after_logits[t] = stage_logits(t)
        if 0 <= t - STAGE_SKEW < n_q:
            after_weights[t - STAGE_SKEW] = stage_weights(after_logits.pop(t - STAGE_SKEW))
        if 0 <= t - 2 * STAGE_SKEW < n_q:
            acc, carry = stage_values(after_weights.pop(t - 2 * STAGE_SKEW))
            accs.append(acc)
            carries.append(carry)

    n_first = TK // TQ + 1
    worst = jnp.max(functools.reduce(jnp.maximum, carries[n_first:]))
    worst_first = jnp.max(functools.reduce(jnp.maximum, carries[:n_first]))
    worst = jnp.where(io > 0, jnp.maximum(worst, worst_first), worst)
    for qi in range(n_q):
        write_out(slice(qi * TQ, (qi + 1) * TQ), accs[qi])
        acc_scr[qi] = accs[qi]
        carry_scr[qi] = jnp.broadcast_to(carries[qi], (2 * TQ, LANES))

    def finish(qi, _):
        q_rows = pl.ds(pl.multiple_of(qi * TQ, TQ), TQ)
        q2 = stack_heads(q_ref[0, q_rows, :])

        def unfinished():
            return jnp.max(carry_scr[qi]) > EXP2_ZERO_BOUND

        def cond(state):
            k_end, go = state
            return jnp.logical_and(k_end > 0, go)

        def body(state):
            k_end, _ = state
            k_start = pl.multiple_of(jnp.maximum(k_end - TK, 0), TQ)
            kblk = k_ref[0, pl.ds(k_start, TK), :]
            vblk = v_ref[0, pl.ds(k_start, TK), :]
            carry = carry_scr[qi]
            pv, c = key_block(q2, kblk, vblk, incl_far, carry[:, 0:1],
                              (far_col + k_start) < k_end)
            acc_scr[qi] += pv
            carry_scr[qi] = carry + c
            return k_end - TK, unfinished()

        lax.while_loop(cond, body, (io * TQ_OUTER + qi * TQ - TK, unfinished()))
        write_out(q_rows, acc_scr[qi])
        return 0

    @pl.when(worst > EXP2_ZERO_BOUND)
    def _():
        lax.fori_loop(0, n_q, finish, 0)


def _attention(q, k, v, cast_weights=(), cast_layer=0):
    B, S, W = q.shape
    n_pairs = W // LANES
    n_io = S // TQ_OUTER
    n_steps = B * n_pairs * n_io
    qspec = pl.BlockSpec((1, TQ_OUTER, LANES), lambda b, p, i: (b, i, p))
    kvspec = pl.BlockSpec((1, S, LANES), lambda b, p, i: (b, 0, p))
    step = lambda b, p, i: (b * n_pairs + p) * n_io + i
    w_in_specs, w_out_specs, w_out_shapes = [], [], []
    for w in cast_weights:
        _, R, C = w.shape
        rows = R // n_steps
        assert rows * n_steps == R and rows % (2 * SUBLANES) == 0, (R, n_steps)
        w_in_specs.append(pl.BlockSpec((None, rows, C),
                                       lambda b, p, i: (cast_layer, step(b, p, i), 0)))
        w_out_specs.append(pl.BlockSpec((rows, C), lambda b, p, i: (step(b, p, i), 0)))
        w_out_shapes.append(jax.ShapeDtypeStruct((R, C), BF16))
    out = pl.pallas_call(
        functools.partial(_attention_body, len(cast_weights)),
        grid=(B, n_pairs, n_io),
        in_specs=[qspec, kvspec, kvspec] + w_in_specs,
        out_specs=[qspec] + w_out_specs,
        out_shape=[jax.ShapeDtypeStruct((B, S, W), BF16)] + w_out_shapes,
        scratch_shapes=[
            pltpu.VMEM((TQ_OUTER // TQ, 2 * TQ, LANES), F32),
            pltpu.VMEM((TQ_OUTER // TQ, 2 * TQ, LANES), F32),
        ],
        compiler_params=pltpu.CompilerParams(
            dimension_semantics=("arbitrary", "arbitrary", "arbitrary"),
            vmem_limit_bytes=VMEM_LIMIT_BYTES),
        name="attention",
    )(q, k, v, *cast_weights)
    return out[0], tuple(out[1:])


def _merge_ffn_body(x_ref, mab_ref, gc_ref, yc_ref, wc_ref, wo_ref, fg_ref, wgu_ref, wd_ref,
                    o_ref, act_scr):
    def stage_branch(st):
        st["ydc"] = _dot(yc_ref[st["rows"], :], wc_ref[...])

    def stage_wo(st):
        rows = st["rows"]
        merged = mab_ref[rows, :].astype(F32) + gc_ref[rows, :].astype(F32) * st["ydc"]
        x1 = x_ref[rows, :] + _dot(merged.astype(BF16), wo_ref[...])
        o_ref[rows, :] = x1
        ms = jnp.mean(x1 * x1, axis=-1, keepdims=True)
        st["h2"] = (x1 * lax.rsqrt(ms + EPS) * fg_ref[...]).astype(BF16)

    def stage_gate_up(st, c0):
        c1 = min(c0 + FF_SPLIT, D_FF)
        gate = _dot(st["h2"], wgu_ref[:, c0:c1])
        up = _dot(st["h2"], wgu_ref[:, D_FF + c0:D_FF + c1])
        act_scr[st["rows"], c0:c1] = (gate * _sigmoid(gate) * up).astype(BF16)

    def stage_down(st):
        rows = st["rows"]
        o_ref[rows, :] += _dot(act_scr[rows, :], wd_ref[...])

    subs = [dict(rows=slice(r0, r0 + TS_FFN)) for r0 in range(0, x_ref.shape[0], TS_FFN)]
    for stage in (stage_branch, stage_wo):
        for st in subs:
            stage(st)
    for c0 in range(0, D_FF, FF_SPLIT):
        for st in subs:
            stage_gate_up(st, c0)
    for st in subs:
        stage_down(st)


def _merge_ffn(x, mab, gc, yc, wbo, wo, fg, wgu, wd):
    M, D = x.shape
    tm = TM_FFN
    tok = lambda width: pl.BlockSpec((tm, width), lambda i: (i, 0))
    return pl.pallas_call(
        _merge_ffn_body,
        grid=(M // tm,),
        in_specs=[
            tok(D), tok(D), tok(D), tok(ATT_WIDTH),
            pl.BlockSpec((None, CONV_CH, D), lambda *_: (2, 0, 0),
                         pipeline_mode=pl.Buffered(1)),
            _resident(wo.shape), _resident(fg.shape),
            _resident(wgu.shape), _resident(wd.shape),
        ],
        out_specs=tok(D),
        out_shape=jax.ShapeDtypeStruct((M, D), F32),
        scratch_shapes=[pltpu.VMEM((tm, D_FF), BF16)],
        compiler_params=pltpu.CompilerParams(
            dimension_semantics=("arbitrary",),
            vmem_limit_bytes=VMEM_LIMIT_BYTES),
        name="merge_ffn",
    )(x, mab, gc, yc, wbo, wo, fg, wgu, wd)


def kernel(x, mix_norm_g, w_in, b_gate, conv_w, conv_b, sgu_ln_g, sgu_ln_b, sgu_w, sgu_b,
           q_norm_g, k_norm_g, w_branch_out, w_o, ffn_norm_g, w_gate_up, w_down):
    B, S, D = x.shape
    depth = w_in.shape[0]
    row = lambda a: a.reshape(1, -1)
    stacked = (w_in, w_branch_out.reshape(depth, N_BRANCH * CONV_CH, D), w_o, w_gate_up, w_down)
    weights = tuple(_cast_bf16(w, 0) for w in stacked)
    for l in range(depth):
        win, wbo, wo, wgu, wd = weights
        wbo = wbo.reshape(N_BRANCH, CONV_CH, D)
        sbf = jnp.repeat(sgu_b[l].T, CHUNK, axis=1)
        qg = jnp.tile(q_norm_g[l], LANES // HEAD_DIM).reshape(1, LANES)
        kg = jnp.tile(k_norm_g[l], LANES // HEAD_DIM).reshape(1, LANES)
        mab, gc, q, k, v = _mixer_in(
            x, row(mix_norm_g[l]), win, row(b_gate[l]),
            conv_w[l], row(conv_b[l]), row(sgu_ln_g[l]), row(sgu_ln_b[l]),
            sgu_w[l], sbf, qg, kg, wbo)
        if l + 1 < depth:
            yc, weights = _attention(q, k, v, stacked, l + 1)
        else:
            yc, _ = _attention(q, k, v)
        x = _merge_ffn(
            x.reshape(B * S, D), mab.reshape(B * S, D), gc.reshape(B * S, D),
            yc.reshape(B * S, ATT_WIDTH), wbo, wo, row(ffn_norm_g[l]), wgu, wd).reshape(B, S, D)
    return x
```

```python
import functools

import jax
import jax.numpy as jnp
from jax import lax
from jax.experimental import pallas as pl
from jax.experimental.pallas import tpu as pltpu

F32 = jnp.float32
BF16 = jnp.bfloat16

D_MODEL = 1024
CONV_CH = 512
CONV_K = 3
SGU_WIDTH = 512
SGU_GROUPS = 4
CHUNK = 128
ATT_HEADS = 8
HEAD_DIM = 64
ATT_WIDTH = ATT_HEADS * HEAD_DIM
N_BRANCH = 3
D_FF = 2816
IN_COLS = 7168
EPS = 1e-6

LANES = 128
SUBLANES = 8
VMEM_LIMIT_BYTES = 56 * 1024 * 1024

C_AB, C_AC, C_AX = 0, 512, 1024
C_SU, C_SV = 1536, 2048
C_Q, C_K, C_V = 2560, 3072, 3584
C_G = 4096

TM_IN = 1024
TS_IN = 512
TM_FFN = 1024
TS_FFN = 512
TQ_OUTER = 4096
TQ = 128
TK = 256
MXU_DIM = 256
FF_SPLIT = 6 * MXU_DIM
CAST_BLOCK_BYTES = 8 * 1024 * 1024

LOG2E = 1.4426950408889634
EXP2_CLAMP = 126.0
STAGE_SKEW = 2

EXP2_ZERO_BOUND = -151.0


def _sigmoid(x):
    return 1.0 / (1.0 + jnp.exp(-x))


def _gelu_exact(x):
    return 0.5 * x * (1.0 + lax.erf(x * (2.0 ** -0.5)))


def _dot(a, b):
    return jnp.dot(a, b, preferred_element_type=F32)


def _resident(shape):
    zeros = (0,) * len(shape)
    return pl.BlockSpec(shape, lambda *_: zeros, pipeline_mode=pl.Buffered(1))


def _cast_body(w_ref, o_ref):
    o_ref[...] = w_ref[...].astype(o_ref.dtype)


def _cast_rows(n_rows, n_cols):
    best = None
    for rows in range(2 * SUBLANES, n_rows + 1, 2 * SUBLANES):
        if n_rows % rows == 0 and rows * n_cols * 4 <= CAST_BLOCK_BYTES:
            best = rows
    assert best is not None, (n_rows, n_cols)
    return best


def _cast_bf16(w, layer):
    _, R, C = w.shape
    rows = _cast_rows(R, C)
    return pl.pallas_call(
        _cast_body,
        grid=(R // rows,),
        in_specs=[pl.BlockSpec((None, rows, C), lambda r: (layer, r, 0))],
        out_specs=pl.BlockSpec((rows, C), lambda r: (r, 0)),
        out_shape=jax.ShapeDtypeStruct((R, C), BF16),
        compiler_params=pltpu.CompilerParams(
            dimension_semantics=("arbitrary",),
            vmem_limit_bytes=VMEM_LIMIT_BYTES),
        name="cast_bf16",
    )(w)


def _head_rmsnorm(p, g128):
    tm = p.shape[0]
    lo = lax.broadcasted_iota(jnp.int32, (tm, LANES), 1) < HEAD_DIM
    outs = []
    for c in range(ATT_WIDTH // LANES):
        pc = p[:, c * LANES:(c + 1) * LANES]
        pp = pc * pc
        s_lo = jnp.sum(jnp.where(lo, pp, 0.0), axis=-1, keepdims=True)
        s_hi = jnp.sum(jnp.where(lo, 0.0, pp), axis=-1, keepdims=True)
        ms = jnp.where(lo, s_lo, s_hi) * (1.0 / HEAD_DIM)
        outs.append(pc * lax.rsqrt(ms + EPS) * g128)
    return jnp.concatenate(outs, axis=-1)


def _mixer_in_body(x_ref, ng_ref, win_ref, bg_ref, cw_ref, cb_ref, lng_ref, lnb_ref,
                   sw_ref, sb_ref, qg_ref, kg_ref, wab_ref,
                   mab_ref, gc_ref, q_ref, k_ref, v_ref,
                   u_scr, yb_scr):
    tm = x_ref.shape[1]
    ts = TS_IN
    j = pl.program_id(1)

    @pl.when(j == 0)
    def _():
        u_scr[0:SUBLANES, :] = jnp.zeros((SUBLANES, CONV_CH), F32)

    row = lax.broadcasted_iota(jnp.int32, (CHUNK, CHUNK), 0)
    col = lax.broadcasted_iota(jnp.int32, (CHUNK, CHUNK), 1)
    sgu_w = [jnp.where(col <= row, sw_ref[g], 0.0).astype(BF16) for g in range(SGU_GROUPS)]

    def proj(st, c0, width):
        return _dot(st["h"], win_ref[:, c0:c0 + width])

    def stage_norm(st):
        x = x_ref[0, st["rows"], :]
        ms = jnp.mean(x * x, axis=-1, keepdims=True)
        st["h"] = (x * lax.rsqrt(ms + EPS) * ng_ref[...]).astype(BF16)

    def stage_conv(st):
        u = proj(st, C_AC, CONV_CH) * proj(st, C_AX, CONV_CH)
        u_scr[SUBLANES:SUBLANES + ts, :] = u
        u1 = u_scr[SUBLANES - 1:SUBLANES - 1 + ts, :]
        u2 = u_scr[SUBLANES - 2:SUBLANES - 2 + ts, :]
        y = cb_ref[...] + cw_ref[0:1, :] * u2
        y = y + cw_ref[1:2, :] * u1
        y = y + cw_ref[2:3, :] * u
        st["ya"] = (proj(st, C_AB, CONV_CH) * y).astype(BF16)
        u_scr[0:SUBLANES, :] = u_scr[ts:ts + SUBLANES, :]

    def stage_sgu_in(st):
        st["gu"] = _gelu_exact(proj(st, C_SU, SGU_WIDTH))
        gv = _gelu_exact(proj(st, C_SV, SGU_WIDTH))
        mu = jnp.mean(gv, axis=-1, keepdims=True)
        vc = gv - mu
        var = jnp.mean(vc * vc, axis=-1, keepdims=True)
        st["vn"] = (vc * lax.rsqrt(var + EPS) * lng_ref[...] + lnb_ref[...]).astype(BF16)

    def stage_qkv(st):
        rows = st["rows"]
        q = _head_rmsnorm(proj(st, C_Q, ATT_WIDTH), qg_ref[...]) * (HEAD_DIM ** -0.5 * LOG2E)
        q_ref[0, rows, :] = q.astype(BF16)
        k_ref[0, rows, :] = _head_rmsnorm(proj(st, C_K, ATT_WIDTH), kg_ref[...]).astype(BF16)
        v_ref[0, rows, :] = proj(st, C_V, ATT_WIDTH).astype(BF16)

    def stage_sgu_mix(st):
        r0 = st["rows"].start
        for g in range(SGU_GROUPS):
            cs = slice(g * CHUNK, (g + 1) * CHUNK)
            for c in range(ts // CHUNK):
                rs = slice(c * CHUNK, (c + 1) * CHUNK)
                mixed = _dot(sgu_w[g], st["vn"][rs, cs]) + sb_ref[:, cs]
                yb_scr[r0 + c * CHUNK:r0 + (c + 1) * CHUNK, cs] = (
                    st["gu"][rs, cs] * mixed).astype(BF16)

    def stage_gates(st):
        rows = st["rows"]
        ga = _sigmoid(proj(st, C_G, D_MODEL) + bg_ref[:, 0:D_MODEL])
        gb = _sigmoid(proj(st, C_G + D_MODEL, D_MODEL) + bg_ref[:, D_MODEL:2 * D_MODEL])
        mab = ga * _dot(st["ya"], wab_ref[0]) + gb * _dot(yb_scr[rows, :], wab_ref[1])
        mab_ref[0, rows, :] = mab.astype(BF16)
        gc = _sigmoid(proj(st, C_G + 2 * D_MODEL, D_MODEL) + bg_ref[:, 2 * D_MODEL:3 * D_MODEL])
        gc_ref[0, rows, :] = gc.astype(BF16)

    subs = [dict(rows=slice(r0, r0 + ts)) for r0 in range(0, tm, ts)]
    for st in subs:
        stage_norm(st)
    for i in range(len(subs) + 1):
        if i < len(subs):
            stage_conv(subs[i])
            stage_sgu_in(subs[i])
            stage_qkv(subs[i])
        if i >= 1:
            stage_sgu_mix(subs[i - 1])
            stage_gates(subs[i - 1])


def _mixer_in(x, ng, win, bg, cw, cb, lng, lnb, sw, sbf, qg, kg, wbo):
    B, S, D = x.shape
    tm = TM_IN
    tok = lambda width: pl.BlockSpec((1, tm, width), lambda b, j: (b, j, 0))
    out_shape = (
        jax.ShapeDtypeStruct((B, S, D), BF16),
        jax.ShapeDtypeStruct((B, S, D), BF16),
        jax.ShapeDtypeStruct((B, S, ATT_WIDTH), BF16),
        jax.ShapeDtypeStruct((B, S, ATT_WIDTH), BF16),
        jax.ShapeDtypeStruct((B, S, ATT_WIDTH), BF16),
    )
    return pl.pallas_call(
        _mixer_in_body,
        grid=(B, S // tm),
        in_specs=[
            tok(D),
            _resident(ng.shape), _resident(win.shape), _resident(bg.shape),
            _resident(cw.shape), _resident(cb.shape), _resident(lng.shape),
            _resident(lnb.shape), _resident(sw.shape), _resident(sbf.shape),
            _resident(qg.shape), _resident(kg.shape),
            _resident((2, CONV_CH, D)),
        ],
        out_specs=(tok(D), tok(D), tok(ATT_WIDTH), tok(ATT_WIDTH), tok(ATT_WIDTH)),
        out_shape=out_shape,
        scratch_shapes=[
            pltpu.VMEM((TS_IN + SUBLANES, CONV_CH), F32),
            pltpu.VMEM((tm, SGU_WIDTH), BF16),
        ],
        compiler_params=pltpu.CompilerParams(
            dimension_semantics=("arbitrary", "arbitrary"),
            vmem_limit_bytes=VMEM_LIMIT_BYTES),
        name="mixer_in",
    )(x, ng, win, bg, cw, cb, lng, lnb, sw, sbf, qg, kg, wbo)


def _attention_body(n_cast, q_ref, k_ref, v_ref, *refs):
    w_refs, o_ref = refs[:n_cast], refs[n_cast]
    wo_refs = refs[n_cast + 1:2 * n_cast + 1]
    acc_scr, carry_scr = refs[2 * n_cast + 1:]
    for w_ref, wo_ref in zip(w_refs, wo_refs):
        wo_ref[...] = w_ref[...].astype(wo_ref.dtype)

    io = pl.program_id(2)
    n_q = TQ_OUTER // TQ
    lo = lax.broadcasted_iota(jnp.int32, (TQ, LANES), 1) < HEAD_DIM

    def neg_inclusive(n):
        return jnp.where(lax.broadcasted_iota(jnp.int32, (n, n), 0)
                         >= lax.broadcasted_iota(jnp.int32, (n, n), 1), -1.0, 0.0).astype(BF16)

    incl_diag = neg_inclusive(TQ)
    incl_far = neg_inclusive(TK)
    diag_mask = (lax.broadcasted_iota(jnp.int32, (2 * TQ, TQ), 1)
                 < lax.broadcasted_iota(jnp.int32, (2 * TQ, TQ), 0) % TQ)
    far_col = lax.broadcasted_iota(jnp.int32, (2 * TQ, TK), 1)

    def stack_heads(q):
        zero = jnp.zeros_like(q)
        return jnp.concatenate([jnp.where(lo, q, zero), jnp.where(lo, zero, q)], axis=0)

    def logits(q2, kblk, mask):
        z = lax.dot_general(q2, kblk, (((1,), (1,)), ((), ())), preferred_element_type=F32)
        sp = jnp.maximum(z, jnp.log(1.0 + jnp.exp2(jnp.minimum(z, EXP2_CLAMP))) * LOG2E)
        if mask is not None:
            sp = jnp.where(mask, sp, 0.0)
        return z, sp.astype(BF16)

    def weights(z, sp, incl, carry, mask):
        neg_cum = _dot(sp, incl)
        e = z + neg_cum
        if carry is not None:
            e = e + carry
        a = jnp.exp2(e)
        if mask is not None:
            a = jnp.where(mask, a, 0.0)
        return a.astype(BF16), neg_cum[:, 0:1]

    def key_block(q2, kblk, vblk, incl, carry, mask):
        z, sp = logits(q2, kblk, mask)
        a, c = weights(z, sp, incl, carry, mask)
        return _dot(a, vblk), c

    def write_out(q_rows, acc):
        o_ref[0, q_rows, :] = jnp.where(lo, acc[:TQ], acc[TQ:]).astype(o_ref.dtype)

    def stage_logits(qi):
        q_start = pl.multiple_of(io * TQ_OUTER + qi * TQ, TQ)
        far_start = pl.multiple_of(jnp.maximum(q_start - TK, 0), TQ)
        q2 = stack_heads(q_ref[0, qi * TQ:(qi + 1) * TQ, :])
        far_mask = (far_col + far_start) < q_start if qi * TQ < TK else None
        z_d, sp_d = logits(q2, k_ref[0, pl.ds(q_start, TQ), :], diag_mask)
        z_f, sp_f = logits(q2, k_ref[0, pl.ds(far_start, TK), :], far_mask)
        return dict(q_start=q_start, far_start=far_start, far_mask=far_mask,
                    z_d=z_d, sp_d=sp_d, z_f=z_f, sp_f=sp_f)

    def stage_weights(st):
        a_d, c_d = weights(st["z_d"], st["sp_d"], incl_diag, None, diag_mask)
        a_f, c_f = weights(st["z_f"], st["sp_f"], incl_far, c_d, st["far_mask"])
        return dict(q_start=st["q_start"], far_start=st["far_start"],
                    a_d=a_d, a_f=a_f, carry=c_d + c_f)

    def stage_values(st):
        pv_d = _dot(st["a_d"], v_ref[0, pl.ds(st["q_start"], TQ), :])
        pv_f = _dot(st["a_f"], v_ref[0, pl.ds(st["far_start"], TK), :])
        return pv_d + pv_f, st["carry"]

    after_logits, after_weights, accs, carries = {}, {}, [], []
    for t in range(n_q + 2 * STAGE_SKEW):
        if t < n_q:
            after_logits[t] = stage_logits(t)
        if 0 <= t - STAGE_SKEW < n_q:
            after_weights[t - STAGE_SKEW] = stage_weights(after_logits.pop(t - STAGE_SKEW))
        if 0 <= t - 2 * STAGE_SKEW < n_q:
            acc, carry = stage_values(after_weights.pop(t - 2 * STAGE_SKEW))
            accs.append(acc)
            carries.append(carry)

    n_first = TK // TQ + 1
    worst = jnp.max(functools.reduce(jnp.maximum, carries[n_first:]))
    worst_first = jnp.max(functools.reduce(jnp.maximum, carries[:n_first]))
    worst = jnp.where(io > 0, jnp.maximum(worst, worst_first), worst)
    for qi in range(n_q):
        write_out(slice(qi * TQ, (qi + 1) * TQ), accs[qi])
        acc_scr[qi] = accs[qi]
        carry_scr[qi] = jnp.broadcast_to(carries[qi], (2 * TQ, LANES))

    def finish(qi, _):
        q_rows = pl.ds(pl.multiple_of(qi * TQ, TQ), TQ)
        q2 = stack_heads(q_ref[0, q_rows, :])

        def unfinished():
            return jnp.max(carry_scr[qi]) > EXP2_ZERO_BOUND

        def cond(state):
            k_end, go = state
            return jnp.logical_and(k_end > 0, go)

        def body(state):
            k_end, _ = state
            k_start = pl.multiple_of(jnp.maximum(k_end - TK, 0), TQ)
            kblk = k_ref[0, pl.ds(k_start, TK), :]
            vblk = v_ref[0, pl.ds(k_start, TK), :]
            carry = carry_scr[qi]
            pv, c = key_block(q2, kblk, vblk, incl_far, carry[:, 0:1],
                              (far_col + k_start) < k_end)
            acc_scr[qi] += pv
            carry_scr[qi] = carry + c
            return k_end - TK, unfinished()

        lax.while_loop(cond, body, (io * TQ_OUTER + qi * TQ - TK, unfinished()))
        write_out(q_rows, acc_scr[qi])
        return 0

    @pl.when(worst > EXP2_ZERO_BOUND)
    def _():
        lax.fori_loop(0, n_q, finish, 0)


def _attention(q, k, v, cast_weights=()):
    B, S, W = q.shape
    n_pairs = W // LANES
    n_io = S // TQ_OUTER
    n_steps = B * n_pairs * n_io
    qspec = pl.BlockSpec((1, TQ_OUTER, LANES), lambda b, p, i: (b, i, p))
    kvspec = pl.BlockSpec((1, S, LANES), lambda b, p, i: (b, 0, p))
    step = lambda b, p, i: (b * n_pairs + p) * n_io + i
    w_in_specs, w_out_specs, w_out_shapes = [], [], []
    for w, layer in cast_weights:
        _, R, C = w.shape
        rows = R // n_steps
        assert rows * n_steps == R and rows % (2 * SUBLANES) == 0, (R, n_steps)
        w_in_specs.append(pl.BlockSpec(
            (None, rows, C), lambda b, p, i, layer=layer: (layer, step(b, p, i), 0)))
        w_out_specs.append(pl.BlockSpec((rows, C), lambda b, p, i: (step(b, p, i), 0)))
        w_out_shapes.append(jax.ShapeDtypeStruct((R, C), BF16))
    out = pl.pallas_call(
        functools.partial(_attention_body, len(cast_weights)),
        grid=(B, n_pairs, n_io),
        in_specs=[qspec, kvspec, kvspec] + w_in_specs,
        out_specs=[qspec] + w_out_specs,
        out_shape=[jax.ShapeDtypeStruct((B, S, W), BF16)] + w_out_shapes,
        scratch_shapes=[
            pltpu.VMEM((TQ_OUTER // TQ, 2 * TQ, LANES), F32),
            pltpu.VMEM((TQ_OUTER // TQ, 2 * TQ, LANES), F32),
        ],
        compiler_params=pltpu.CompilerParams(
            dimension_semantics=("arbitrary", "arbitrary", "arbitrary"),
            vmem_limit_bytes=VMEM_LIMIT_BYTES),
        name="attention",
    )(q, k, v, *(w for w, _ in cast_weights))
    return out[0], tuple(out[1:])


def _merge_ffn_body(x_ref, mab_ref, gc_ref, yc_ref, wc_ref, wo_ref, fg_ref, wgu_ref, wd_ref,
                    o_ref, act_scr):
    def stage_branch(st):
        st["ydc"] = _dot(yc_ref[st["rows"], :], wc_ref[...])

    def stage_wo(st):
        rows = st["rows"]
        merged = mab_ref[rows, :].astype(F32) + gc_ref[rows, :].astype(F32) * st["ydc"]
        x1 = x_ref[rows, :] + _dot(merged.astype(BF16), wo_ref[...])
        o_ref[rows, :] = x1
        ms = jnp.mean(x1 * x1, axis=-1, keepdims=True)
        st["h2"] = (x1 * lax.rsqrt(ms + EPS) * fg_ref[...]).astype(BF16)

    def stage_gate_up(st, c0):
        c1 = min(c0 + FF_SPLIT, D_FF)
        gate = _dot(st["h2"], wgu_ref[:, c0:c1])
        up = _dot(st["h2"], wgu_ref[:, D_FF + c0:D_FF + c1])
        act_scr[st["rows"], c0:c1] = (gate * _sigmoid(gate) * up).astype(BF16)

    def stage_down(st):
        rows = st["rows"]
        o_ref[rows, :] += _dot(act_scr[rows, :], wd_ref[...])

    subs = [dict(rows=slice(r0, r0 + TS_FFN)) for r0 in range(0, x_ref.shape[0], TS_FFN)]
    for stage in (stage_branch, stage_wo):
        for st in subs:
            stage(st)
    for c0 in range(0, D_FF, FF_SPLIT):
        for st in subs:
            stage_gate_up(st, c0)
    for st in subs:
        stage_down(st)


def _merge_ffn(x, mab, gc, yc, wbo, wo, fg, wgu, wd):
    M, D = x.shape
    tm = TM_FFN
    tok = lambda width: pl.BlockSpec((tm, width), lambda i: (i, 0))
    return pl.pallas_call(
        _merge_ffn_body,
        grid=(M // tm,),
        in_specs=[
            tok(D), tok(D), tok(D), tok(ATT_WIDTH),
            pl.BlockSpec((None, CONV_CH, D), lambda *_: (2, 0, 0),
                         pipeline_mode=pl.Buffered(1)),
            _resident(wo.shape), _resident(fg.shape),
            _resident(wgu.shape), _resident(wd.shape),
        ],
        out_specs=tok(D),
        out_shape=jax.ShapeDtypeStruct((M, D), F32),
        scratch_shapes=[pltpu.VMEM((tm, D_FF), BF16)],
        compiler_params=pltpu.CompilerParams(
            dimension_semantics=("arbitrary",),
            vmem_limit_bytes=VMEM_LIMIT_BYTES),
        name="merge_ffn",
    )(x, mab, gc, yc, wbo, wo, fg, wgu, wd)


def kernel(x, mix_norm_g, w_in, b_gate, conv_w, conv_b, sgu_ln_g, sgu_ln_b, sgu_w, sgu_b,
           q_norm_g, k_norm_g, w_branch_out, w_o, ffn_norm_g, w_gate_up, w_down):
    B, S, D = x.shape
    depth = w_in.shape[0]
    row = lambda a: a.reshape(1, -1)
    w_bo = w_branch_out.reshape(depth, N_BRANCH * CONV_CH, D)
    win, wbo = _cast_bf16(w_in, 0), _cast_bf16(w_bo, 0)
    for l in range(depth):
        wbo = wbo.reshape(N_BRANCH, CONV_CH, D)
        sbf = jnp.repeat(sgu_b[l].T, CHUNK, axis=1)
        qg = jnp.tile(q_norm_g[l], LANES // HEAD_DIM).reshape(1, LANES)
        kg = jnp.tile(k_norm_g[l], LANES // HEAD_DIM).reshape(1, LANES)
        mab, gc, q, k, v = _mixer_in(
            x, row(mix_norm_g[l]), win, row(b_gate[l]),
            conv_w[l], row(conv_b[l]), row(sgu_ln_g[l]), row(sgu_ln_b[l]),
            sgu_w[l], sbf, qg, kg, wbo)
        casts = [(w_o, l), (w_gate_up, l), (w_down, l)]
        if l + 1 < depth:
            casts += [(w_in, l + 1), (w_bo, l + 1)]
        yc, cast = _attention(q, k, v, casts)
        wo, wgu, wd = cast[:3]
        x = _merge_ffn(
            x.reshape(B * S, D), mab.reshape(B * S, D), gc.reshape(B * S, D),
            yc.reshape(B * S, ATT_WIDTH), wbo, wo, row(ffn_norm_g[l]), wgu, wd).reshape(B, S, D)
        if l + 1 < depth:
            win, wbo = cast[3:]
    return x
```

```python
import functools

import jax
import jax.numpy as jnp
from jax import lax
from jax.experimental import pallas as pl
from jax.experimental.pallas import tpu as pltpu

F32 = jnp.float32
BF16 = jnp.bfloat16

D_MODEL = 1024
CONV_CH = 512
CONV_K = 3
SGU_WIDTH = 512
SGU_GROUPS = 4
CHUNK = 128
ATT_HEADS = 8
HEAD_DIM = 64
ATT_WIDTH = ATT_HEADS * HEAD_DIM
N_BRANCH = 3
D_FF = 2816
IN_COLS = 7168
EPS = 1e-6

LANES = 128
SUBLANES = 8
VMEM_LIMIT_BYTES = 56 * 1024 * 1024

C_AB, C_AC, C_AX = 0, CONV_CH, 2 * CONV_CH
C_SU = 3 * CONV_CH
C_SV = C_SU + SGU_WIDTH
C_Q = C_SV + SGU_WIDTH
C_K, C_V = C_Q + ATT_WIDTH, C_Q + 2 * ATT_WIDTH
C_G = C_Q + 3 * ATT_WIDTH
assert C_G + N_BRANCH * D_MODEL == IN_COLS

TM_IN = 1024
TS_IN = 512
TM_FFN = 1024
TS_FFN = 512
TQ_OUTER = 4096
TQ = 128
TK = 256
MXU_DIM = 256
FF_SPLIT = 6 * MXU_DIM
CAST_BLOCK_BYTES = 8 * 1024 * 1024

LOG2E = 1.4426950408889634
EXP2_CLAMP = 126.0
STAGE_SKEW = 2

EXP2_ZERO_BOUND = -151.0


def _sigmoid(x):
    return 1.0 / (1.0 + jnp.exp(-x))


def _gelu_exact(x):
    return 0.5 * x * (1.0 + lax.erf(x * (2.0 ** -0.5)))


def _dot(a, b):
    return jnp.dot(a, b, preferred_element_type=F32)


def _resident(shape):
    zeros = (0,) * len(shape)
    return pl.BlockSpec(shape, lambda *_: zeros, pipeline_mode=pl.Buffered(1))


def _resident_layer(shape, layer):
    index = (layer,) + (0,) * (len(shape) - 1)
    return pl.BlockSpec((None,) + tuple(shape[1:]), lambda *_: index,
                        pipeline_mode=pl.Buffered(1))


def _cast_body(w_ref, o_ref):
    o_ref[...] = w_ref[...].astype(o_ref.dtype)


def _cast_rows(n_rows, n_cols):
    best = None
    for rows in range(2 * SUBLANES, n_rows + 1, 2 * SUBLANES):
        if n_rows % rows == 0 and rows * n_cols * 4 <= CAST_BLOCK_BYTES:
            best = rows
    assert best is not None, (n_rows, n_cols)
    return best


def _cast_bf16(w, layer):
    _, R, C = w.shape
    rows = _cast_rows(R, C)
    return pl.pallas_call(
        _cast_body,
        grid=(R // rows,),
        in_specs=[pl.BlockSpec((None, rows, C), lambda r: (layer, r, 0))],
        out_specs=pl.BlockSpec((rows, C), lambda r: (r, 0)),
        out_shape=jax.ShapeDtypeStruct((R, C), BF16),
        compiler_params=pltpu.CompilerParams(
            dimension_semantics=("arbitrary",),
            vmem_limit_bytes=VMEM_LIMIT_BYTES),
        name="cast_bf16",
    )(w)


def _head_rmsnorm(p, g128):
    tm = p.shape[0]
    lo = lax.broadcasted_iota(jnp.int32, (tm, LANES), 1) < HEAD_DIM
    outs = []
    for c in range(ATT_WIDTH // LANES):
        pc = p[:, c * LANES:(c + 1) * LANES]
        pp = pc * pc
        s_lo = jnp.sum(jnp.where(lo, pp, 0.0), axis=-1, keepdims=True)
        s_hi = jnp.sum(jnp.where(lo, 0.0, pp), axis=-1, keepdims=True)
        ms = jnp.where(lo, s_lo, s_hi) * (1.0 / HEAD_DIM)
        outs.append(pc * lax.rsqrt(ms + EPS) * g128)
    return jnp.concatenate(outs, axis=-1)


def _mixer_in_body(layer, x_ref, ng_ref, win_ref, bg_ref, cw_ref, cb_ref, lng_ref, lnb_ref,
                   sw_ref, sb_ref, qg_ref, kg_ref, wab_ref,
                   mab_ref, gc_ref, q_ref, k_ref, v_ref,
                   u_scr, yb_scr):
    tm = x_ref.shape[1]
    ts = TS_IN
    j = pl.program_id(1)
    lrow = slice(layer, layer + 1)

    @pl.when(j == 0)
    def _():
        u_scr[0:SUBLANES, :] = jnp.zeros((SUBLANES, CONV_CH), F32)

    row = lax.broadcasted_iota(jnp.int32, (CHUNK, CHUNK), 0)
    col = lax.broadcasted_iota(jnp.int32, (CHUNK, CHUNK), 1)
    sgu_w = [jnp.where(col <= row, sw_ref[g], 0.0).astype(BF16) for g in range(SGU_GROUPS)]

    def proj(st, c0, width):
        return _dot(st["h"], win_ref[:, c0:c0 + width])

    def stage_norm(st):
        x = x_ref[0, st["rows"], :]
        ms = jnp.mean(x * x, axis=-1, keepdims=True)
        st["h"] = (x * lax.rsqrt(ms + EPS) * ng_ref[lrow, :]).astype(BF16)

    def stage_conv(st):
        u = proj(st, C_AC, CONV_CH) * proj(st, C_AX, CONV_CH)
        u_scr[SUBLANES:SUBLANES + ts, :] = u
        u1 = u_scr[SUBLANES - 1:SUBLANES - 1 + ts, :]
        u2 = u_scr[SUBLANES - 2:SUBLANES - 2 + ts, :]
        y = cb_ref[lrow, :] + cw_ref[0:1, :] * u2
        y = y + cw_ref[1:2, :] * u1
        y = y + cw_ref[2:3, :] * u
        st["ya"] = (proj(st, C_AB, CONV_CH) * y).astype(BF16)
        u_scr[0:SUBLANES, :] = u_scr[ts:ts + SUBLANES, :]

    def stage_sgu_in(st):
        st["gu"] = _gelu_exact(proj(st, C_SU, SGU_WIDTH))
        gv = _gelu_exact(proj(st, C_SV, SGU_WIDTH))
        mu = jnp.mean(gv, axis=-1, keepdims=True)
        vc = gv - mu
        var = jnp.mean(vc * vc, axis=-1, keepdims=True)
        st["vn"] = (vc * lax.rsqrt(var + EPS) * lng_ref[lrow, :] + lnb_ref[lrow, :]).astype(BF16)

    def stage_qkv(st):
        rows = st["rows"]
        q = _head_rmsnorm(proj(st, C_Q, ATT_WIDTH), qg_ref[lrow, :]) * (HEAD_DIM ** -0.5 * LOG2E)
        q_ref[0, rows, :] = q.astype(BF16)
        k_ref[0, rows, :] = _head_rmsnorm(proj(st, C_K, ATT_WIDTH), kg_ref[lrow, :]).astype(BF16)
        v_ref[0, rows, :] = proj(st, C_V, ATT_WIDTH).astype(BF16)

    def stage_sgu_mix(st):
        r0 = st["rows"].start
        for g in range(SGU_GROUPS):
            cs = slice(g * CHUNK, (g + 1) * CHUNK)
            for c in range(ts // CHUNK):
                rs = slice(c * CHUNK, (c + 1) * CHUNK)
                mixed = _dot(sgu_w[g], st["vn"][rs, cs]) + sb_ref[:, cs]
                yb_scr[r0 + c * CHUNK:r0 + (c + 1) * CHUNK, cs] = (
                    st["gu"][rs, cs] * mixed).astype(BF16)

    def stage_gates(st):
        rows = st["rows"]
        ga = _sigmoid(proj(st, C_G, D_MODEL) + bg_ref[lrow, 0:D_MODEL])
        gb = _sigmoid(proj(st, C_G + D_MODEL, D_MODEL) + bg_ref[lrow, D_MODEL:2 * D_MODEL])
        mab = ga * _dot(st["ya"], wab_ref[0]) + gb * _dot(yb_scr[rows, :], wab_ref[1])
        mab_ref[0, rows, :] = mab.astype(BF16)
        gc = _sigmoid(proj(st, C_G + 2 * D_MODEL, D_MODEL) + bg_ref[lrow, 2 * D_MODEL:3 * D_MODEL])
        gc_ref[0, rows, :] = gc.astype(BF16)

    subs = [dict(rows=slice(r0, r0 + ts)) for r0 in range(0, tm, ts)]
    for st in subs:
        stage_norm(st)
    for i in range(len(subs) + 1):
        if i < len(subs):
            stage_conv(subs[i])
            stage_sgu_in(subs[i])
            stage_qkv(subs[i])
        if i >= 1:
            stage_sgu_mix(subs[i - 1])
            stage_gates(subs[i - 1])


def _mixer_in(l, x, ng, win, bg, cw, cb, lng, lnb, sw, sbf, qg, kg, wbo):
    B, S, D = x.shape
    tm = TM_IN
    tok = lambda width: pl.BlockSpec((1, tm, width), lambda b, j: (b, j, 0))
    out_shape = (
        jax.ShapeDtypeStruct((B, S, D), BF16),
        jax.ShapeDtypeStruct((B, S, D), BF16),
        jax.ShapeDtypeStruct((B, S, ATT_WIDTH), BF16),
        jax.ShapeDtypeStruct((B, S, ATT_WIDTH), BF16),
        jax.ShapeDtypeStruct((B, S, ATT_WIDTH), BF16),
    )
    return pl.pallas_call(
        functools.partial(_mixer_in_body, l),
        grid=(B, S // tm),
        in_specs=[
            tok(D),
            _resident(ng.shape), _resident(win.shape), _resident(bg.shape),
            _resident_layer(cw.shape, l), _resident(cb.shape),
            _resident(lng.shape), _resident(lnb.shape),
            _resident_layer(sw.shape, l), _resident_layer(sbf.shape, l),
            _resident(qg.shape), _resident(kg.shape),
            _resident((2, CONV_CH, D)),
        ],
        out_specs=(tok(D), tok(D), tok(ATT_WIDTH), tok(ATT_WIDTH), tok(ATT_WIDTH)),
        out_shape=out_shape,
        scratch_shapes=[
            pltpu.VMEM((TS_IN + SUBLANES, CONV_CH), F32),
            pltpu.VMEM((tm, SGU_WIDTH), BF16),
        ],
        compiler_params=pltpu.CompilerParams(
            dimension_semantics=("arbitrary", "arbitrary"),
            vmem_limit_bytes=VMEM_LIMIT_BYTES),
        name="mixer_in",
    )(x, ng, win, bg, cw, cb, lng, lnb, sw, sbf, qg, kg, wbo)


def _attention_body(n_cast, q_ref, k_ref, v_ref, *refs):
    w_refs, o_ref = refs[:n_cast], refs[n_cast]
    wo_refs = refs[n_cast + 1:2 * n_cast + 1]
    acc_scr, carry_scr = refs[2 * n_cast + 1:]
    for w_ref, wo_ref in zip(w_refs, wo_refs):
        wo_ref[...] = w_ref[...].astype(wo_ref.dtype)

    io = pl.program_id(2)
    n_q = TQ_OUTER // TQ
    lo = lax.broadcasted_iota(jnp.int32, (TQ, LANES), 1) < HEAD_DIM

    def neg_inclusive(n):
        return jnp.where(lax.broadcasted_iota(jnp.int32, (n, n), 0)
                         >= lax.broadcasted_iota(jnp.int32, (n, n), 1), -1.0, 0.0).astype(BF16)

    incl_diag = neg_inclusive(TQ)
    incl_far = neg_inclusive(TK)
    diag_mask = (lax.broadcasted_iota(jnp.int32, (2 * TQ, TQ), 1)
                 < lax.broadcasted_iota(jnp.int32, (2 * TQ, TQ), 0) % TQ)
    far_col = lax.broadcasted_iota(jnp.int32, (2 * TQ, TK), 1)

    def stack_heads(q):
        zero = jnp.zeros_like(q)
        return jnp.concatenate([jnp.where(lo, q, zero), jnp.where(lo, zero, q)], axis=0)

    def logits(q2, kblk, mask):
        z = lax.dot_general(q2, kblk, (((1,), (1,)), ((), ())), preferred_element_type=F32)
        sp = jnp.maximum(z, jnp.log(1.0 + jnp.exp2(jnp.minimum(z, EXP2_CLAMP))) * LOG2E)
        if mask is not None:
            sp = jnp.where(mask, sp, 0.0)
        return z, sp.astype(BF16)

    def weights(z, sp, incl, carry, mask):
        neg_cum = _dot(sp, incl)
        e = z + neg_cum
        if carry is not None:
            e = e + carry
        a = jnp.exp2(e)
        if mask is not None:
            a = jnp.where(mask, a, 0.0)
        return a.astype(BF16), neg_cum[:, 0:1]

    def key_block(q2, kblk, vblk, incl, carry, mask):
        z, sp = logits(q2, kblk, mask)
        a, c = weights(z, sp, incl, carry, mask)
        return _dot(a, vblk), c

    def write_out(q_rows, acc):
        o_ref[0, q_rows, :] = jnp.where(lo, acc[:TQ], acc[TQ:]).astype(o_ref.dtype)

    def stage_logits(qi):
        q_start = pl.multiple_of(io * TQ_OUTER + qi * TQ, TQ)
        far_start = pl.multiple_of(jnp.maximum(q_start - TK, 0), TQ)
        q2 = stack_heads(q_ref[0, qi * TQ:(qi + 1) * TQ, :])
        far_mask = (far_col + far_start) < q_start if qi * TQ < TK else None
        z_d, sp_d = logits(q2, k_ref[0, pl.ds(q_start, TQ), :], diag_mask)
        z_f, sp_f = logits(q2, k_ref[0, pl.ds(far_start, TK), :], far_mask)
        return dict(q_start=q_start, far_start=far_start, far_mask=far_mask,
                    z_d=z_d, sp_d=sp_d, z_f=z_f, sp_f=sp_f)

    def stage_weights(st):
        a_d, c_d = weights(st["z_d"], st["sp_d"], incl_diag, None, diag_mask)
        a_f, c_f = weights(st["z_f"], st["sp_f"], incl_far, c_d, st["far_mask"])
        return dict(q_start=st["q_start"], far_start=st["far_start"],
                    a_d=a_d, a_f=a_f, carry=c_d + c_f)

    def stage_values(st):
        pv_d = _dot(st["a_d"], v_ref[0, pl.ds(st["q_start"], TQ), :])
        pv_f = _dot(st["a_f"], v_ref[0, pl.ds(st["far_start"], TK), :])
        return pv_d + pv_f, st["carry"]

    after_logits, after_weights, accs, carries = {}, {}, [], []
    for t in range(n_q + 2 * STAGE_SKEW):
        if t < n_q:
            after_logits[t] = stage_logits(t)
        if 0 <= t - STAGE_SKEW < n_q:
            after_weights[t - STAGE_SKEW] = stage_weights(after_logits.pop(t - STAGE_SKEW))
        if 0 <= t - 2 * STAGE_SKEW < n_q:
            acc, carry = stage_values(after_weights.pop(t - 2 * STAGE_SKEW))
            accs.append(acc)
            carries.append(carry)

    n_first = TK // TQ + 1
    worst = jnp.max(functools.reduce(jnp.maximum, carries[n_first:]))
    worst_first = jnp.max(functools.reduce(jnp.maximum, carries[:n_first]))
    worst = jnp.where(io > 0, jnp.maximum(worst, worst_first), worst)
    for qi in range(n_q):
        write_out(slice(qi * TQ, (qi + 1) * TQ), accs[qi])
        acc_scr[qi] = accs[qi]
        carry_scr[qi] = jnp.broadcast_to(carries[qi], (2 * TQ, LANES))

    def finish(qi, _):
        q_rows = pl.ds(pl.multiple_of(qi * TQ, TQ), TQ)
        q2 = stack_heads(q_ref[0, q_rows, :])

        def unfinished():
            return jnp.max(carry_scr[qi]) > EXP2_ZERO_BOUND

        def cond(state):
            k_end, go = state
            return jnp.logical_and(k_end > 0, go)

        def body(state):
            k_end, _ = state
            k_start = pl.multiple_of(jnp.maximum(k_end - TK, 0), TQ)
            kblk = k_ref[0, pl.ds(k_start, TK), :]
            vblk = v_ref[0, pl.ds(k_start, TK), :]
            carry = carry_scr[qi]
            pv, c = key_block(q2, kblk, vblk, incl_far, carry[:, 0:1],
                              (far_col + k_start) < k_end)
            acc_scr[qi] += pv
            carry_scr[qi] = carry + c
            return k_end - TK, unfinished()

        lax.while_loop(cond, body, (io * TQ_OUTER + qi * TQ - TK, unfinished()))
        write_out(q_rows, acc_scr[qi])
        return 0

    @pl.when(worst > EXP2_ZERO_BOUND)
    def _():
        lax.fori_loop(0, n_q, finish, 0)


def _attention(q, k, v, cast_weights=()):
    B, S, W = q.shape
    n_pairs = W // LANES
    n_io = S // TQ_OUTER
    n_steps = B * n_pairs * n_io
    qspec = pl.BlockSpec((1, TQ_OUTER, LANES), lambda b, p, i: (b, i, p))
    kvspec = pl.BlockSpec((1, S, LANES), lambda b, p, i: (b, 0, p))
    step = lambda b, p, i: (b * n_pairs + p) * n_io + i
    w_in_specs, w_out_specs, w_out_shapes = [], [], []
    for w, layer in cast_weights:
        _, R, C = w.shape
        rows = R // n_steps
        assert rows * n_steps == R and rows % (2 * SUBLANES) == 0, (R, n_steps)
        w_in_specs.append(pl.BlockSpec(
            (None, rows, C), lambda b, p, i, layer=layer: (layer, step(b, p, i), 0)))
        w_out_specs.append(pl.BlockSpec((rows, C), lambda b, p, i: (step(b, p, i), 0)))
        w_out_shapes.append(jax.ShapeDtypeStruct((R, C), BF16))
    out = pl.pallas_call(
        functools.partial(_attention_body, len(cast_weights)),
        grid=(B, n_pairs, n_io),
        in_specs=[qspec, kvspec, kvspec] + w_in_specs,
        out_specs=[qspec] + w_out_specs,
        out_shape=[jax.ShapeDtypeStruct((B, S, W), BF16)] + w_out_shapes,
        scratch_shapes=[
            pltpu.VMEM((TQ_OUTER // TQ, 2 * TQ, LANES), F32),
            pltpu.VMEM((TQ_OUTER // TQ, 2 * TQ, LANES), F32),
        ],
        compiler_params=pltpu.CompilerParams(
            dimension_semantics=("arbitrary", "arbitrary", "arbitrary"),
            vmem_limit_bytes=VMEM_LIMIT_BYTES),
        name="attention",
    )(q, k, v, *(w for w, _ in cast_weights))
    return out[0], tuple(out[1:])


def _merge_ffn_body(layer, x_ref, mab_ref, gc_ref, yc_ref, wc_ref, wo_ref, fg_ref, wgu_ref, wd_ref,
                    o_ref, act_scr):
    def stage_branch(st):
        st["ydc"] = _dot(yc_ref[st["rows"], :], wc_ref[...])

    def stage_wo(st):
        rows = st["rows"]
        merged = mab_ref[rows, :].astype(F32) + gc_ref[rows, :].astype(F32) * st["ydc"]
        x1 = x_ref[rows, :] + _dot(merged.astype(BF16), wo_ref[...])
        o_ref[rows, :] = x1
        ms = jnp.mean(x1 * x1, axis=-1, keepdims=True)
        st["h2"] = (x1 * lax.rsqrt(ms + EPS) * fg_ref[layer:layer + 1, :]).astype(BF16)

    def stage_gate_up(st, c0):
        c1 = min(c0 + FF_SPLIT, D_FF)
        gate = _dot(st["h2"], wgu_ref[:, c0:c1])
        up = _dot(st["h2"], wgu_ref[:, D_FF + c0:D_FF + c1])
        act_scr[st["rows"], c0:c1] = (gate * _sigmoid(gate) * up).astype(BF16)

    def stage_down(st):
        rows = st["rows"]
        o_ref[rows, :] += _dot(act_scr[rows, :], wd_ref[...])

    subs = [dict(rows=slice(r0, r0 + TS_FFN)) for r0 in range(0, x_ref.shape[0], TS_FFN)]
    for stage in (stage_branch, stage_wo):
        for st in subs:
            stage(st)
    for c0 in range(0, D_FF, FF_SPLIT):
        for st in subs:
            stage_gate_up(st, c0)
    for st in subs:
        stage_down(st)


def _merge_ffn(l, x, mab, gc, yc, wbo, wo, fg, wgu, wd):
    M, D = x.shape
    tm = TM_FFN
    tok = lambda width: pl.BlockSpec((tm, width), lambda i: (i, 0))
    return pl.pallas_call(
        functools.partial(_merge_ffn_body, l),
        grid=(M // tm,),
        in_specs=[
            tok(D), tok(D), tok(D), tok(ATT_WIDTH),
            pl.BlockSpec((None, CONV_CH, D), lambda *_: (2, 0, 0),
                         pipeline_mode=pl.Buffered(1)),
            _resident(wo.shape), _resident(fg.shape),
            _resident(wgu.shape), _resident(wd.shape),
        ],
        out_specs=tok(D),
        out_shape=jax.ShapeDtypeStruct((M, D), F32),
        scratch_shapes=[pltpu.VMEM((tm, D_FF), BF16)],
        compiler_params=pltpu.CompilerParams(
            dimension_semantics=("arbitrary",),
            vmem_limit_bytes=VMEM_LIMIT_BYTES),
        name="merge_ffn",
    )(x, mab, gc, yc, wbo, wo, fg, wgu, wd)


def kernel(x, mix_norm_g, w_in, b_gate, conv_w, conv_b, sgu_ln_g, sgu_ln_b, sgu_w, sgu_b,
           q_norm_g, k_norm_g, w_branch_out, w_o, ffn_norm_g, w_gate_up, w_down):
    B, S, D = x.shape
    depth = w_in.shape[0]
    assert D == D_MODEL and w_in.shape[1:] == (D_MODEL, IN_COLS), (x.shape, w_in.shape)
    assert conv_w.shape[1:] == (CONV_K, CONV_CH) and CONV_K == 3, conv_w.shape
    assert sgu_w.shape[1:] == (SGU_GROUPS, CHUNK, CHUNK), sgu_w.shape
    assert w_gate_up.shape[1:] == (D_MODEL, 2 * D_FF) and w_down.shape[1:] == (D_FF, D_MODEL)
    assert S % TQ_OUTER == 0 and S % TM_IN == 0 and (B * S) % TM_FFN == 0, (B, S)
    sbf = jnp.repeat(jnp.swapaxes(sgu_b, 1, 2), CHUNK, axis=2)
    qg = jnp.tile(q_norm_g, (1, LANES // HEAD_DIM))
    kg = jnp.tile(k_norm_g, (1, LANES // HEAD_DIM))
    w_bo = w_branch_out.reshape(depth, N_BRANCH * CONV_CH, D)
    win, wbo = _cast_bf16(w_in, 0), _cast_bf16(w_bo, 0)
    for l in range(depth):
        wbo = wbo.reshape(N_BRANCH, CONV_CH, D)
        mab, gc, q, k, v = _mixer_in(
            l, x, mix_norm_g, win, b_gate, conv_w, conv_b, sgu_ln_g, sgu_ln_b, sgu_w, sbf,
            qg, kg, wbo)
        casts = [(w_o, l), (w_gate_up, l), (w_down, l)]
        if l + 1 < depth:
            casts += [(w_in, l + 1), (w_bo, l + 1)]
        yc, cast = _attention(q, k, v, casts)
        wo, wgu, wd = cast[:3]
        x = _merge_ffn(
            l, x.reshape(B * S, D), mab.reshape(B * S, D), gc.reshape(B * S, D),
            yc.reshape(B * S, ATT_WIDTH), wbo, wo, ffn_norm_g, wgu, wd).reshape(B, S, D)
        if l + 1 < depth:
            win, wbo = cast[3:]
    return x
```

```python
import functools

import jax
import jax.numpy as jnp
from jax import lax
from jax.experimental import pallas as pl
from jax.experimental.pallas import tpu as pltpu

F32 = jnp.float32
BF16 = jnp.bfloat16

D_MODEL = 1024
CONV_CH = 512
CONV_K = 3
SGU_WIDTH = 512
SGU_GROUPS = 4
CHUNK = 128
ATT_HEADS = 8
HEAD_DIM = 64
ATT_WIDTH = ATT_HEADS * HEAD_DIM
N_BRANCH = 3
D_FF = 2816
IN_COLS = 7168
EPS = 1e-6

LANES = 128
SUBLANES = 8
VMEM_LIMIT_BYTES = 56 * 1024 * 1024

C_AB, C_AC, C_AX = 0, CONV_CH, 2 * CONV_CH
C_SU = 3 * CONV_CH
C_SV = C_SU + SGU_WIDTH
C_Q = C_SV + SGU_WIDTH
C_K, C_V = C_Q + ATT_WIDTH, C_Q + 2 * ATT_WIDTH
C_G = C_Q + 3 * ATT_WIDTH
assert C_G + N_BRANCH * D_MODEL == IN_COLS

TM_IN = 1024
TS_IN = 512
TM_FFN = 1024
TS_FFN = 512
TQ_OUTER = 4096
TQ = 128
TK = 256
MXU_DIM = 256
FF_SPLIT = 6 * MXU_DIM
CAST_BLOCK_BYTES = 8 * 1024 * 1024

LOG2E = 1.4426950408889634
EXP2_CLAMP = 126.0
STAGE_SKEW = 2

EXP2_ZERO_BOUND = -151.0


def _sigmoid(x):
    return 1.0 / (1.0 + jnp.exp(-x))


def _gelu_exact(x):
    return 0.5 * x * (1.0 + lax.erf(x * (2.0 ** -0.5)))


def _dot(a, b):
    return jnp.dot(a, b, preferred_element_type=F32)


def _resident(shape):
    zeros = (0,) * len(shape)
    return pl.BlockSpec(shape, lambda *_: zeros, pipeline_mode=pl.Buffered(1))


def _resident_layer(shape, layer):
    index = (layer,) + (0,) * (len(shape) - 1)
    return pl.BlockSpec((None,) + tuple(shape[1:]), lambda *_: index,
                        pipeline_mode=pl.Buffered(1))


def _cast_body(w_ref, o_ref):
    o_ref[...] = w_ref[...].astype(o_ref.dtype)


def _cast_rows(n_rows, n_cols):
    best = None
    for rows in range(2 * SUBLANES, n_rows + 1, 2 * SUBLANES):
        if n_rows % rows == 0 and rows * n_cols * 4 <= CAST_BLOCK_BYTES:
            best = rows
    assert best is not None, (n_rows, n_cols)
    return best


def _cast_bf16(w, layer):
    _, R, C = w.shape
    rows = _cast_rows(R, C)
    return pl.pallas_call(
        _cast_body,
        grid=(R // rows,),
        in_specs=[pl.BlockSpec((None, rows, C), lambda r: (layer, r, 0))],
        out_specs=pl.BlockSpec((rows, C), lambda r: (r, 0)),
        out_shape=jax.ShapeDtypeStruct((R, C), BF16),
        compiler_params=pltpu.CompilerParams(
            dimension_semantics=("arbitrary",),
            vmem_limit_bytes=VMEM_LIMIT_BYTES),
        name="cast_bf16",
    )(w)


def _head_rmsnorm(p, g128):
    tm = p.shape[0]
    lo = lax.broadcasted_iota(jnp.int32, (tm, LANES), 1) < HEAD_DIM
    outs = []
    for c in range(ATT_WIDTH // LANES):
        pc = p[:, c * LANES:(c + 1) * LANES]
        pp = pc * pc
        s_lo = jnp.sum(jnp.where(lo, pp, 0.0), axis=-1, keepdims=True)
        s_hi = jnp.sum(jnp.where(lo, 0.0, pp), axis=-1, keepdims=True)
        ms = jnp.where(lo, s_lo, s_hi) * (1.0 / HEAD_DIM)
        outs.append(pc * lax.rsqrt(ms + EPS) * g128)
    return jnp.concatenate(outs, axis=-1)


def _mixer_in_body(layer, x_ref, ng_ref, win_ref, bg_ref, cw_ref, cb_ref, lng_ref, lnb_ref,
                   sw_ref, sb_ref, qg_ref, kg_ref, wab_ref,
                   mab_ref, gc_ref, q_ref, k_ref, v_ref,
                   u_scr, yb_scr):
    tm = x_ref.shape[1]
    ts = TS_IN
    j = pl.program_id(1)
    lrow = slice(layer, layer + 1)

    @pl.when(j == 0)
    def _():
        u_scr[0:SUBLANES, :] = jnp.zeros((SUBLANES, CONV_CH), F32)

    row = lax.broadcasted_iota(jnp.int32, (CHUNK, CHUNK), 0)
    col = lax.broadcasted_iota(jnp.int32, (CHUNK, CHUNK), 1)
    sgu_w = [jnp.where(col <= row, sw_ref[g], 0.0).astype(BF16) for g in range(SGU_GROUPS)]

    def proj(st, c0, width):
        return _dot(st["h"], win_ref[:, c0:c0 + width])

    def stage_norm(st):
        x = x_ref[0, st["rows"], :]
        ms = jnp.mean(x * x, axis=-1, keepdims=True)
        st["h"] = (x * lax.rsqrt(ms + EPS) * ng_ref[lrow, :]).astype(BF16)

    def stage_conv(st):
        u = proj(st, C_AC, CONV_CH) * proj(st, C_AX, CONV_CH)
        u_scr[SUBLANES:SUBLANES + ts, :] = u
        u1 = u_scr[SUBLANES - 1:SUBLANES - 1 + ts, :]
        u2 = u_scr[SUBLANES - 2:SUBLANES - 2 + ts, :]
        y = cb_ref[lrow, :] + cw_ref[0:1, :] * u2
        y = y + cw_ref[1:2, :] * u1
        y = y + cw_ref[2:3, :] * u
        st["ya"] = (proj(st, C_AB, CONV_CH) * y).astype(BF16)
        u_scr[0:SUBLANES, :] = u_scr[ts:ts + SUBLANES, :]

    def stage_sgu_in(st):
        st["gu"] = _gelu_exact(proj(st, C_SU, SGU_WIDTH))
        gv = _gelu_exact(proj(st, C_SV, SGU_WIDTH))
        mu = jnp.mean(gv, axis=-1, keepdims=True)
        vc = gv - mu
        var = jnp.mean(vc * vc, axis=-1, keepdims=True)
        st["vn"] = (vc * lax.rsqrt(var + EPS) * lng_ref[lrow, :] + lnb_ref[lrow, :]).astype(BF16)

    def stage_qkv(st):
        rows = st["rows"]
        q = _head_rmsnorm(proj(st, C_Q, ATT_WIDTH), qg_ref[lrow, :]) * (HEAD_DIM ** -0.5 * LOG2E)
        q_ref[0, rows, :] = q.astype(BF16)
        k_ref[0, rows, :] = _head_rmsnorm(proj(st, C_K, ATT_WIDTH), kg_ref[lrow, :]).astype(BF16)
        v_ref[0, rows, :] = proj(st, C_V, ATT_WIDTH).astype(BF16)

    def stage_sgu_mix(st):
        r0 = st["rows"].start
        for g in range(SGU_GROUPS):
            cs = slice(g * CHUNK, (g + 1) * CHUNK)
            for c in range(ts // CHUNK):
                rs = slice(c * CHUNK, (c + 1) * CHUNK)
                mixed = _dot(sgu_w[g], st["vn"][rs, cs]) + sb_ref[:, cs]
                yb_scr[r0 + c * CHUNK:r0 + (c + 1) * CHUNK, cs] = (
                    st["gu"][rs, cs] * mixed).astype(BF16)

    def stage_gates(st):
        rows = st["rows"]
        ga = _sigmoid(proj(st, C_G, D_MODEL) + bg_ref[lrow, 0:D_MODEL])
        gb = _sigmoid(proj(st, C_G + D_MODEL, D_MODEL) + bg_ref[lrow, D_MODEL:2 * D_MODEL])
        mab = ga * _dot(st["ya"], wab_ref[0]) + gb * _dot(yb_scr[rows, :], wab_ref[1])
        mab_ref[0, rows, :] = mab.astype(BF16)
        gc = _sigmoid(proj(st, C_G + 2 * D_MODEL, D_MODEL) + bg_ref[lrow, 2 * D_MODEL:3 * D_MODEL])
        gc_ref[0, rows, :] = gc.astype(BF16)

    subs = [dict(rows=slice(r0, r0 + ts)) for r0 in range(0, tm, ts)]
    for st in subs:
        stage_norm(st)
    for i in range(len(subs) + 1):
        if i < len(subs):
            stage_conv(subs[i])
            stage_sgu_in(subs[i])
            stage_qkv(subs[i])
        if i >= 1:
            stage_sgu_mix(subs[i - 1])
            stage_gates(subs[i - 1])


def _mixer_in(l, x, ng, win, bg, cw, cb, lng, lnb, sw, sbf, qg, kg, wbo):
    B, S, D = x.shape
    tm = TM_IN
    tok = lambda width: pl.BlockSpec((1, tm, width), lambda b, j: (b, j, 0))
    out_shape = (
        jax.ShapeDtypeStruct((B, S, D), BF16),
        jax.ShapeDtypeStruct((B, S, D), BF16),
        jax.ShapeDtypeStruct((B, S, ATT_WIDTH), BF16),
        jax.ShapeDtypeStruct((B, S, ATT_WIDTH), BF16),
        jax.ShapeDtypeStruct((B, S, ATT_WIDTH), BF16),
    )
    return pl.pallas_call(
        functools.partial(_mixer_in_body, l),
        grid=(B, S // tm),
        in_specs=[
            tok(D),
            _resident(ng.shape), _resident(win.shape), _resident(bg.shape),
            _resident_layer(cw.shape, l), _resident(cb.shape),
            _resident(lng.shape), _resident(lnb.shape),
            _resident_layer(sw.shape, l), _resident_layer(sbf.shape, l),
            _resident(qg.shape), _resident(kg.shape),
            _resident((2, CONV_CH, D)),
        ],
        out_specs=(tok(D), tok(D), tok(ATT_WIDTH), tok(ATT_WIDTH), tok(ATT_WIDTH)),
        out_shape=out_shape,
        scratch_shapes=[
            pltpu.VMEM((TS_IN + SUBLANES, CONV_CH), F32),
            pltpu.VMEM((tm, SGU_WIDTH), BF16),
        ],
        compiler_params=pltpu.CompilerParams(
            dimension_semantics=("arbitrary", "arbitrary"),
            vmem_limit_bytes=VMEM_LIMIT_BYTES),
        name="mixer_in",
    )(x, ng, win, bg, cw, cb, lng, lnb, sw, sbf, qg, kg, wbo)


def _attention_body(n_cast, q_ref, k_ref, v_ref, *refs):
    w_refs, o_ref = refs[:n_cast], refs[n_cast]
    wo_refs = refs[n_cast + 1:2 * n_cast + 1]
    acc_scr, carry_scr = refs[2 * n_cast + 1:]
    for w_ref, wo_ref in zip(w_refs, wo_refs):
        wo_ref[...] = w_ref[...].astype(wo_ref.dtype)

    io = pl.program_id(2)
    n_q = TQ_OUTER // TQ
    lo = lax.broadcasted_iota(jnp.int32, (TQ, LANES), 1) < HEAD_DIM

    def neg_inclusive(n):
        return jnp.where(lax.broadcasted_iota(jnp.int32, (n, n), 0)
                         >= lax.broadcasted_iota(jnp.int32, (n, n), 1), -1.0, 0.0).astype(BF16)

    incl_diag = jnp.concatenate([neg_inclusive(TQ), jnp.full((TQ, LANES), -1.0, BF16)], axis=1)
    incl_far = neg_inclusive(TK)
    diag_mask = (lax.broadcasted_iota(jnp.int32, (2 * TQ, TQ), 1)
                 < lax.broadcasted_iota(jnp.int32, (2 * TQ, TQ), 0) % TQ)
    far_col = lax.broadcasted_iota(jnp.int32, (2 * TQ, TK), 1)

    def stack_heads(q):
        zero = jnp.zeros_like(q)
        return jnp.concatenate([jnp.where(lo, q, zero), jnp.where(lo, zero, q)], axis=0)

    def logits(q2, kblk, mask):
        z = lax.dot_general(q2, kblk, (((1,), (1,)), ((), ())), preferred_element_type=F32)
        sp = jnp.maximum(z, jnp.log(1.0 + jnp.exp2(jnp.minimum(z, EXP2_CLAMP))) * LOG2E)
        if mask is not None:
            sp = jnp.where(mask, sp, 0.0)
        return z, sp.astype(BF16)

    def weights(z, neg_cum, carry, mask):
        e = z + neg_cum
        if carry is not None:
            e = e + carry
        a = jnp.exp2(e)
        if mask is not None:
            a = jnp.where(mask, a, 0.0)
        return a.astype(BF16)

    def write_out(q_rows, acc):
        o_ref[0, q_rows, :] = jnp.where(lo, acc[:TQ], acc[TQ:]).astype(o_ref.dtype)

    def stage_logits(qi):
        q_start = pl.multiple_of(io * TQ_OUTER + qi * TQ, TQ)
        far_start = pl.multiple_of(jnp.maximum(q_start - TK, 0), TQ)
        q2 = stack_heads(q_ref[0, qi * TQ:(qi + 1) * TQ, :])
        far_mask = (far_col + far_start) < q_start if qi * TQ < TK else None
        z_d, sp_d = logits(q2, k_ref[0, pl.ds(q_start, TQ), :], diag_mask)
        z_f, sp_f = logits(q2, k_ref[0, pl.ds(far_start, TK), :], far_mask)
        return dict(q_start=q_start, far_start=far_start, far_mask=far_mask,
                    z_d=z_d, sp_d=sp_d, z_f=z_f, sp_f=sp_f)

    def stage_weights(st):
        neg_d = _dot(st["sp_d"], incl_diag)
        neg_f = _dot(st["sp_f"], incl_far)
        carry_d = neg_d[:, TQ:]
        a_d = weights(st["z_d"], neg_d[:, :TQ], None, diag_mask)
        a_f = weights(st["z_f"], neg_f, jnp.concatenate([carry_d] * (TK // LANES), axis=1),
                      st["far_mask"])
        return dict(q_start=st["q_start"], far_start=st["far_start"],
                    a_d=a_d, a_f=a_f, carry=carry_d[:, 0:1] + neg_f[:, 0:1])

    def stage_values(st):
        pv_d = _dot(st["a_d"], v_ref[0, pl.ds(st["q_start"], TQ), :])
        pv_f = _dot(st["a_f"], v_ref[0, pl.ds(st["far_start"], TK), :])
        return pv_d + pv_f, st["carry"]

    after_logits, after_weights, accs, carries = {}, {}, [], []
    for t in range(n_q + 2 * STAGE_SKEW):
        if t < n_q:
            after_logits[t] = stage_logits(t)
        if 0 <= t - STAGE_SKEW < n_q:
            after_weights[t - STAGE_SKEW] = stage_weights(after_logits.pop(t - STAGE_SKEW))
        if 0 <= t - 2 * STAGE_SKEW < n_q:
            acc, carry = stage_values(after_weights.pop(t - 2 * STAGE_SKEW))
            accs.append(acc)
            carries.append(carry)

    n_first = TK // TQ + 1
    worst = jnp.max(functools.reduce(jnp.maximum, carries[n_first:]))
    worst_first = jnp.max(functools.reduce(jnp.maximum, carries[:n_first]))
    worst = jnp.where(io > 0, jnp.maximum(worst, worst_first), worst)
    for qi in range(n_q):
        write_out(slice(qi * TQ, (qi + 1) * TQ), accs[qi])
        acc_scr[qi] = accs[qi]
        carry_scr[qi] = carries[qi]

    def finish(qi, _):
        q_rows = pl.ds(pl.multiple_of(qi * TQ, TQ), TQ)
        q2 = stack_heads(q_ref[0, q_rows, :])

        def unfinished():
            return jnp.max(carry_scr[qi]) > EXP2_ZERO_BOUND

        def cond(state):
            k_end, go = state
            return jnp.logical_and(k_end > 0, go)

        def body(state):
            k_end, _ = state
            k_start = pl.multiple_of(jnp.maximum(k_end - TK, 0), TQ)
            kblk = k_ref[0, pl.ds(k_start, TK), :]
            vblk = v_ref[0, pl.ds(k_start, TK), :]
            carry = carry_scr[qi]
            mask = (far_col + k_start) < k_end
            z, sp = logits(q2, kblk, mask)
            neg_cum = _dot(sp, incl_far)
            acc_scr[qi] += _dot(weights(z, neg_cum, carry, mask), vblk)
            carry_scr[qi] = carry + neg_cum[:, 0:1]
            return k_end - TK, unfinished()

        lax.while_loop(cond, body, (io * TQ_OUTER + qi * TQ - TK, unfinished()))
        write_out(q_rows, acc_scr[qi])
        return 0

    @pl.when(worst > EXP2_ZERO_BOUND)
    def _():
        lax.fori_loop(0, n_q, finish, 0)


def _attention(q, k, v, cast_weights=()):
    B, S, W = q.shape
    n_pairs = W // LANES
    n_io = S // TQ_OUTER
    n_steps = B * n_pairs * n_io
    qspec = pl.BlockSpec((1, TQ_OUTER, LANES), lambda b, p, i: (b, i, p))
    kvspec = pl.BlockSpec((1, S, LANES), lambda b, p, i: (b, 0, p))
    step = lambda b, p, i: (b * n_pairs + p) * n_io + i
    w_in_specs, w_out_specs, w_out_shapes = [], [], []
    for w, layer in cast_weights:
        _, R, C = w.shape
        rows = R // n_steps
        assert rows * n_steps == R and rows % (2 * SUBLANES) == 0, (R, n_steps)
        w_in_specs.append(pl.BlockSpec(
            (None, rows, C), lambda b, p, i, layer=layer: (layer, step(b, p, i), 0)))
        w_out_specs.append(pl.BlockSpec((rows, C), lambda b, p, i: (step(b, p, i), 0)))
        w_out_shapes.append(jax.ShapeDtypeStruct((R, C), BF16))
    out = pl.pallas_call(
        functools.partial(_attention_body, len(cast_weights)),
        grid=(B, n_pairs, n_io),
        in_specs=[qspec, kvspec, kvspec] + w_in_specs,
        out_specs=[qspec] + w_out_specs,
        out_shape=[jax.ShapeDtypeStruct((B, S, W), BF16)] + w_out_shapes,
        scratch_shapes=[
            pltpu.VMEM((TQ_OUTER // TQ, 2 * TQ, LANES), F32),
            pltpu.VMEM((TQ_OUTER // TQ, 2 * TQ, 1), F32),
        ],
        compiler_params=pltpu.CompilerParams(
            dimension_semantics=("arbitrary", "arbitrary", "arbitrary"),
            vmem_limit_bytes=VMEM_LIMIT_BYTES),
        name="attention",
    )(q, k, v, *(w for w, _ in cast_weights))
    return out[0], tuple(out[1:])


def _merge_ffn_body(layer, x_ref, mab_ref, gc_ref, yc_ref, wc_ref, wo_ref, fg_ref, wgu_ref, wd_ref,
                    o_ref, act_scr):
    def stage_branch(st):
        st["ydc"] = _dot(yc_ref[st["rows"], :], wc_ref[...])

    def stage_wo(st):
        rows = st["rows"]
        merged = mab_ref[rows, :].astype(F32) + gc_ref[rows, :].astype(F32) * st["ydc"]
        x1 = x_ref[rows, :] + _dot(merged.astype(BF16), wo_ref[...])
        o_ref[rows, :] = x1
        ms = jnp.mean(x1 * x1, axis=-1, keepdims=True)
        st["h2"] = (x1 * lax.rsqrt(ms + EPS) * fg_ref[layer:layer + 1, :]).astype(BF16)

    def stage_gate_up(st, c0):
        c1 = min(c0 + FF_SPLIT, D_FF)
        gate = _dot(st["h2"], wgu_ref[:, c0:c1])
        up = _dot(st["h2"], wgu_ref[:, D_FF + c0:D_FF + c1])
        act_scr[st["rows"], c0:c1] = (gate * _sigmoid(gate) * up).astype(BF16)

    def stage_down(st):
        rows = st["rows"]
        o_ref[rows, :] += _dot(act_scr[rows, :], wd_ref[...])

    subs = [dict(rows=slice(r0, r0 + TS_FFN)) for r0 in range(0, x_ref.shape[0], TS_FFN)]
    for stage in (stage_branch, stage_wo):
        for st in subs:
            stage(st)
    for c0 in range(0, D_FF, FF_SPLIT):
        for st in subs:
            stage_gate_up(st, c0)
    for st in subs:
        stage_down(st)


def _merge_ffn(l, x, mab, gc, yc, wbo, wo, fg, wgu, wd):
    M, D = x.shape
    tm = TM_FFN
    tok = lambda width: pl.BlockSpec((tm, width), lambda i: (i, 0))
    return pl.pallas_call(
        functools.partial(_merge_ffn_body, l),
        grid=(M // tm,),
        in_specs=[
            tok(D), tok(D), tok(D), tok(ATT_WIDTH),
            pl.BlockSpec((None, CONV_CH, D), lambda *_: (2, 0, 0),
                         pipeline_mode=pl.Buffered(1)),
            _resident(wo.shape), _resident(fg.shape),
            _resident(wgu.shape), _resident(wd.shape),
        ],
        out_specs=tok(D),
        out_shape=jax.ShapeDtypeStruct((M, D), F32),
        scratch_shapes=[pltpu.VMEM((tm, D_FF), BF16)],
        compiler_params=pltpu.CompilerParams(
            dimension_semantics=("arbitrary",),
            vmem_limit_bytes=VMEM_LIMIT_BYTES),
        name="merge_ffn",
    )(x, mab, gc, yc, wbo, wo, fg, wgu, wd)


def kernel(x, mix_norm_g, w_in, b_gate, conv_w, conv_b, sgu_ln_g, sgu_ln_b, sgu_w, sgu_b,
           q_norm_g, k_norm_g, w_branch_out, w_o, ffn_norm_g, w_gate_up, w_down):
    B, S, D = x.shape
    depth = w_in.shape[0]
    assert D == D_MODEL and w_in.shape[1:] == (D_MODEL, IN_COLS), (x.shape, w_in.shape)
    assert conv_w.shape[1:] == (CONV_K, CONV_CH) and CONV_K == 3, conv_w.shape
    assert sgu_w.shape[1:] == (SGU_GROUPS, CHUNK, CHUNK), sgu_w.shape
    assert w_gate_up.shape[1:] == (D_MODEL, 2 * D_FF) and w_down.shape[1:] == (D_FF, D_MODEL)
    assert S % TQ_OUTER == 0 and S % TM_IN == 0 and (B * S) % TM_FFN == 0, (B, S)
    sbf = jnp.repeat(jnp.swapaxes(sgu_b, 1, 2), CHUNK, axis=2)
    qg = jnp.tile(q_norm_g, (1, LANES // HEAD_DIM))
    kg = jnp.tile(k_norm_g, (1, LANES // HEAD_DIM))
    w_bo = w_branch_out.reshape(depth, N_BRANCH * CONV_CH, D)
    win, wbo = _cast_bf16(w_in, 0), _cast_bf16(w_bo, 0)
    for l in range(depth):
        wbo = wbo.reshape(N_BRANCH, CONV_CH, D)
        mab, gc, q, k, v = _mixer_in(
            l, x, mix_norm_g, win, b_gate, conv_w, conv_b, sgu_ln_g, sgu_ln_b, sgu_w, sbf,
            qg, kg, wbo)
        casts = [(w_o, l), (w_gate_up, l), (w_down, l)]
        if l + 1 < depth:
            casts += [(w_in, l + 1), (w_bo, l + 1)]
        yc, cast = _attention(q, k, v, casts)
        wo, wgu, wd = cast[:3]
        x = _merge_ffn(
            l, x.reshape(B * S, D), mab.reshape(B * S, D), gc.reshape(B * S, D),
            yc.reshape(B * S, ATT_WIDTH), wbo, wo, ffn_norm_g, wgu, wd).reshape(B, S, D)
        if l + 1 < depth:
            win, wbo = cast[3:]
    return x
```

```python
import functools

import jax
import jax.numpy as jnp
from jax import lax
from jax.experimental import pallas as pl
from jax.experimental.pallas import tpu as pltpu

F32 = jnp.float32
BF16 = jnp.bfloat16

D_MODEL = 1024
CONV_CH = 512
CONV_K = 3
SGU_WIDTH = 512
SGU_GROUPS = 4
CHUNK = 128
ATT_HEADS = 8
HEAD_DIM = 64
ATT_WIDTH = ATT_HEADS * HEAD_DIM
N_BRANCH = 3
D_FF = 2816
IN_COLS = 7168
EPS = 1e-6

LANES = 128
SUBLANES = 8
VMEM_LIMIT_BYTES = 56 * 1024 * 1024

C_AB, C_AC, C_AX = 0, CONV_CH, 2 * CONV_CH
C_SU = 3 * CONV_CH
C_SV = C_SU + SGU_WIDTH
C_Q = C_SV + SGU_WIDTH
C_K, C_V = C_Q + ATT_WIDTH, C_Q + 2 * ATT_WIDTH
C_G = C_Q + 3 * ATT_WIDTH
assert C_G + N_BRANCH * D_MODEL == IN_COLS

TM_IN = 1024
TS_IN = 512
TM_FFN = 1024
TS_FFN = 512
TQ_OUTER = 4096
TQ = 128
TK = 256
MXU_DIM = 256
FF_SPLIT = 6 * MXU_DIM
CAST_BLOCK_BYTES = 8 * 1024 * 1024

LOG2E = 1.4426950408889634
EXP2_CLAMP = 126.0
STAGE_SKEW = 2

EXP2_ZERO_BOUND = -151.0


def _sigmoid(x):
    return 1.0 / (1.0 + jnp.exp(-x))


def _gelu_exact(x):
    return 0.5 * x * (1.0 + lax.erf(x * (2.0 ** -0.5)))


def _dot(a, b):
    return jnp.dot(a, b, preferred_element_type=F32)


def _resident(shape):
    zeros = (0,) * len(shape)
    return pl.BlockSpec(shape, lambda *_: zeros, pipeline_mode=pl.Buffered(1))


def _resident_layer(shape, layer):
    index = (layer,) + (0,) * (len(shape) - 1)
    return pl.BlockSpec((None,) + tuple(shape[1:]), lambda *_: index,
                        pipeline_mode=pl.Buffered(1))


def _cast_body(w_ref, o_ref):
    o_ref[...] = w_ref[...].astype(o_ref.dtype)


def _cast_rows(n_rows, n_cols):
    best = None
    for rows in range(2 * SUBLANES, n_rows + 1, 2 * SUBLANES):
        if n_rows % rows == 0 and rows * n_cols * 4 <= CAST_BLOCK_BYTES:
            best = rows
    assert best is not None, (n_rows, n_cols)
    return best


def _cast_bf16(w, layer):
    _, R, C = w.shape
    rows = _cast_rows(R, C)
    return pl.pallas_call(
        _cast_body,
        grid=(R // rows,),
        in_specs=[pl.BlockSpec((None, rows, C), lambda r: (layer, r, 0))],
        out_specs=pl.BlockSpec((rows, C), lambda r: (r, 0)),
        out_shape=jax.ShapeDtypeStruct((R, C), BF16),
        compiler_params=pltpu.CompilerParams(
            dimension_semantics=("arbitrary",),
            vmem_limit_bytes=VMEM_LIMIT_BYTES),
        name="cast_bf16",
    )(w)


def _head_rmsnorm(p, g128):
    tm = p.shape[0]
    lo = lax.broadcasted_iota(jnp.int32, (tm, LANES), 1) < HEAD_DIM
    outs = []
    for c in range(ATT_WIDTH // LANES):
        pc = p[:, c * LANES:(c + 1) * LANES]
        pp = pc * pc
        s_lo = jnp.sum(jnp.where(lo, pp, 0.0), axis=-1, keepdims=True)
        s_hi = jnp.sum(jnp.where(lo, 0.0, pp), axis=-1, keepdims=True)
        ms = jnp.where(lo, s_lo, s_hi) * (1.0 / HEAD_DIM)
        outs.append(pc * lax.rsqrt(ms + EPS) * g128)
    return jnp.concatenate(outs, axis=-1)


def _mixer_in_body(layer, x_ref, ng_ref, win_ref, bg_ref, cw_ref, cb_ref, lng_ref, lnb_ref,
                   sw_ref, sb_ref, qg_ref, kg_ref, wab_ref,
                   mab_ref, gc_ref, q_ref, k_ref, v_ref,
                   u_scr, yb_scr):
    tm = x_ref.shape[1]
    ts = TS_IN
    j = pl.program_id(1)
    lrow = slice(layer, layer + 1)

    @pl.when(j == 0)
    def _():
        u_scr[0:SUBLANES, :] = jnp.zeros((SUBLANES, CONV_CH), F32)

    row = lax.broadcasted_iota(jnp.int32, (CHUNK, CHUNK), 0)
    col = lax.broadcasted_iota(jnp.int32, (CHUNK, CHUNK), 1)
    sgu_w = [jnp.where(col <= row, sw_ref[g], 0.0).astype(BF16) for g in range(SGU_GROUPS)]

    def proj(st, c0, width):
        return _dot(st["h"], win_ref[:, c0:c0 + width])

    def stage_norm(st):
        x = x_ref[0, st["rows"], :]
        ms = jnp.mean(x * x, axis=-1, keepdims=True)
        st["h"] = (x * lax.rsqrt(ms + EPS) * ng_ref[lrow, :]).astype(BF16)

    def stage_conv(st):
        u = proj(st, C_AC, CONV_CH) * proj(st, C_AX, CONV_CH)
        u_scr[SUBLANES:SUBLANES + ts, :] = u
        u1 = u_scr[SUBLANES - 1:SUBLANES - 1 + ts, :]
        u2 = u_scr[SUBLANES - 2:SUBLANES - 2 + ts, :]
        y = cb_ref[lrow, :] + cw_ref[0:1, :] * u2
        y = y + cw_ref[1:2, :] * u1
        y = y + cw_ref[2:3, :] * u
        st["ya"] = (proj(st, C_AB, CONV_CH) * y).astype(BF16)
        u_scr[0:SUBLANES, :] = u_scr[ts:ts + SUBLANES, :]

    def stage_sgu_in(st):
        st["gu"] = _gelu_exact(proj(st, C_SU, SGU_WIDTH))
        gv = _gelu_exact(proj(st, C_SV, SGU_WIDTH))
        mu = jnp.mean(gv, axis=-1, keepdims=True)
        vc = gv - mu
        var = jnp.mean(vc * vc, axis=-1, keepdims=True)
        st["vn"] = (vc * lax.rsqrt(var + EPS) * lng_ref[lrow, :] + lnb_ref[lrow, :]).astype(BF16)

    def stage_qkv(st):
        rows = st["rows"]
        q = _head_rmsnorm(proj(st, C_Q, ATT_WIDTH), qg_ref[lrow, :]) * (HEAD_DIM ** -0.5 * LOG2E)
        q_ref[0, rows, :] = q.astype(BF16)
        k_ref[0, rows, :] = _head_rmsnorm(proj(st, C_K, ATT_WIDTH), kg_ref[lrow, :]).astype(BF16)
        v_ref[0, rows, :] = proj(st, C_V, ATT_WIDTH).astype(BF16)

    def stage_sgu_mix(st):
        r0 = st["rows"].start
        for g in range(SGU_GROUPS):
            cs = slice(g * CHUNK, (g + 1) * CHUNK)
            for c in range(ts // CHUNK):
                rs = slice(c * CHUNK, (c + 1) * CHUNK)
                mixed = _dot(sgu_w[g], st["vn"][rs, cs]) + sb_ref[:, cs]
                yb_scr[r0 + c * CHUNK:r0 + (c + 1) * CHUNK, cs] = (
                    st["gu"][rs, cs] * mixed).astype(BF16)

    def stage_gates(st):
        rows = st["rows"]
        ga = _sigmoid(proj(st, C_G, D_MODEL) + bg_ref[lrow, 0:D_MODEL])
        gb = _sigmoid(proj(st, C_G + D_MODEL, D_MODEL) + bg_ref[lrow, D_MODEL:2 * D_MODEL])
        gc = _sigmoid(proj(st, C_G + 2 * D_MODEL, D_MODEL) + bg_ref[lrow, 2 * D_MODEL:3 * D_MODEL])
        gc_ref[0, rows, :] = gc.astype(BF16)
        mab = ga * _dot(st["ya"], wab_ref[0]) + gb * _dot(yb_scr[rows, :], wab_ref[1])
        mab_ref[0, rows, :] = mab.astype(BF16)

    subs = [dict(rows=slice(r0, r0 + ts)) for r0 in range(0, tm, ts)]
    for st in subs:
        stage_norm(st)
    for i in range(len(subs) + 1):
        if i < len(subs):
            stage_conv(subs[i])
            stage_sgu_in(subs[i])
            stage_qkv(subs[i])
        if i >= 1:
            stage_sgu_mix(subs[i - 1])
            stage_gates(subs[i - 1])


def _mixer_in(l, x, ng, win, bg, cw, cb, lng, lnb, sw, sbf, qg, kg, wbo):
    B, S, D = x.shape
    tm = TM_IN
    tok = lambda width: pl.BlockSpec((1, tm, width), lambda b, j: (b, j, 0))
    out_shape = (
        jax.ShapeDtypeStruct((B, S, D), BF16),
        jax.ShapeDtypeStruct((B, S, D), BF16),
        jax.ShapeDtypeStruct((B, S, ATT_WIDTH), BF16),
        jax.ShapeDtypeStruct((B, S, ATT_WIDTH), BF16),
        jax.ShapeDtypeStruct((B, S, ATT_WIDTH), BF16),
    )
    return pl.pallas_call(
        functools.partial(_mixer_in_body, l),
        grid=(B, S // tm),
        in_specs=[
            tok(D),
            _resident(ng.shape), _resident(win.shape), _resident(bg.shape),
            _resident_layer(cw.shape, l), _resident(cb.shape),
            _resident(lng.shape), _resident(lnb.shape),
            _resident_layer(sw.shape, l), _resident_layer(sbf.shape, l),
            _resident(qg.shape), _resident(kg.shape),
            _resident((2, CONV_CH, D)),
        ],
        out_specs=(tok(D), tok(D), tok(ATT_WIDTH), tok(ATT_WIDTH), tok(ATT_WIDTH)),
        out_shape=out_shape,
        scratch_shapes=[
            pltpu.VMEM((TS_IN + SUBLANES, CONV_CH), F32),
            pltpu.VMEM((tm, SGU_WIDTH), BF16),
        ],
        compiler_params=pltpu.CompilerParams(
            dimension_semantics=("arbitrary", "arbitrary"),
            vmem_limit_bytes=VMEM_LIMIT_BYTES),
        name="mixer_in",
    )(x, ng, win, bg, cw, cb, lng, lnb, sw, sbf, qg, kg, wbo)


def _attention_body(n_cast, q_ref, k_ref, v_ref, *refs):
    w_refs, o_ref = refs[:n_cast], refs[n_cast]
    wo_refs = refs[n_cast + 1:2 * n_cast + 1]
    acc_scr, carry_scr = refs[2 * n_cast + 1:]
    for w_ref, wo_ref in zip(w_refs, wo_refs):
        wo_ref[...] = w_ref[...].astype(wo_ref.dtype)

    io = pl.program_id(2)
    n_q = TQ_OUTER // TQ
    lo = lax.broadcasted_iota(jnp.int32, (TQ, LANES), 1) < HEAD_DIM

    def neg_inclusive(n):
        return jnp.where(lax.broadcasted_iota(jnp.int32, (n, n), 0)
                         >= lax.broadcasted_iota(jnp.int32, (n, n), 1), -1.0, 0.0).astype(BF16)

    incl_diag = jnp.concatenate([neg_inclusive(TQ), jnp.full((TQ, LANES), -1.0, BF16)], axis=1)
    incl_far = neg_inclusive(TK)
    diag_mask = (lax.broadcasted_iota(jnp.int32, (2 * TQ, TQ), 1)
                 < lax.broadcasted_iota(jnp.int32, (2 * TQ, TQ), 0) % TQ)
    far_col = lax.broadcasted_iota(jnp.int32, (2 * TQ, TK), 1)

    def stack_heads(q):
        zero = jnp.zeros_like(q)
        return jnp.concatenate([jnp.where(lo, q, zero), jnp.where(lo, zero, q)], axis=0)

    def logits(q2, kblk, mask):
        z = lax.dot_general(q2, kblk, (((1,), (1,)), ((), ())), preferred_element_type=F32)
        sp = jnp.maximum(z, jnp.log(1.0 + jnp.exp2(jnp.minimum(z, EXP2_CLAMP))) * LOG2E)
        if mask is not None:
            sp = jnp.where(mask, sp, 0.0)
        return z, sp.astype(BF16)

    def weights(z, neg_cum, carry, mask):
        e = z + neg_cum
        if carry is not None:
            e = e + carry
        a = jnp.exp2(e)
        if mask is not None:
            a = jnp.where(mask, a, 0.0)
        return a.astype(BF16)

    def write_out(q_rows, acc):
        o_ref[0, q_rows, :] = jnp.where(lo, acc[:TQ], acc[TQ:]).astype(o_ref.dtype)

    def stage_logits(qi):
        q_start = pl.multiple_of(io * TQ_OUTER + qi * TQ, TQ)
        far_start = pl.multiple_of(jnp.maximum(q_start - TK, 0), TQ)
        q2 = stack_heads(q_ref[0, qi * TQ:(qi + 1) * TQ, :])
        far_mask = (far_col + far_start) < q_start if qi * TQ < TK else None
        z_d, sp_d = logits(q2, k_ref[0, pl.ds(q_start, TQ), :], diag_mask)
        z_f, sp_f = logits(q2, k_ref[0, pl.ds(far_start, TK), :], far_mask)
        return dict(q_start=q_start, far_start=far_start, far_mask=far_mask,
                    z_d=z_d, sp_d=sp_d, z_f=z_f, sp_f=sp_f)

    def stage_weights(st):
        neg_d = _dot(st["sp_d"], incl_diag)
        neg_f = _dot(st["sp_f"], incl_far)
        carry_d = neg_d[:, TQ:]
        a_d = weights(st["z_d"], neg_d[:, :TQ], None, diag_mask)
        a_f = weights(st["z_f"], neg_f, jnp.concatenate([carry_d] * (TK // LANES), axis=1),
                      st["far_mask"])
        return dict(q_start=st["q_start"], far_start=st["far_start"],
                    a_d=a_d, a_f=a_f, carry=carry_d[:, 0:1] + neg_f[:, 0:1])

    def stage_values(st):
        pv_d = _dot(st["a_d"], v_ref[0, pl.ds(st["q_start"], TQ), :])
        pv_f = _dot(st["a_f"], v_ref[0, pl.ds(st["far_start"], TK), :])
        return pv_d + pv_f, st["carry"]

    after_logits, after_weights, accs, carries = {}, {}, [], []
    for t in range(n_q + 2 * STAGE_SKEW):
        if t < n_q:
            after_logits[t] = stage_logits(t)
        if 0 <= t - STAGE_SKEW < n_q:
            after_weights[t - STAGE_SKEW] = stage_weights(after_logits.pop(t - STAGE_SKEW))
        if 0 <= t - 2 * STAGE_SKEW < n_q:
            acc, carry = stage_values(after_weights.pop(t - 2 * STAGE_SKEW))
            accs.append(acc)
            carries.append(carry)

    n_first = TK // TQ + 1
    worst = jnp.max(functools.reduce(jnp.maximum, carries[n_first:]))
    worst_first = jnp.max(functools.reduce(jnp.maximum, carries[:n_first]))
    worst = jnp.where(io > 0, jnp.maximum(worst, worst_first), worst)
    for qi in range(n_q):
        write_out(slice(qi * TQ, (qi + 1) * TQ), accs[qi])
        acc_scr[qi] = accs[qi]
        carry_scr[qi] = carries[qi]

    def finish(qi, _):
        q_rows = pl.ds(pl.multiple_of(qi * TQ, TQ), TQ)
        q2 = stack_heads(q_ref[0, q_rows, :])

        def unfinished():
            return jnp.max(carry_scr[qi]) > EXP2_ZERO_BOUND

        def cond(state):
            k_end, go = state
            return jnp.logical_and(k_end > 0, go)

        def body(state):
            k_end, _ = state
            k_start = pl.multiple_of(jnp.maximum(k_end - TK, 0), TQ)
            kblk = k_ref[0, pl.ds(k_start, TK), :]
            vblk = v_ref[0, pl.ds(k_start, TK), :]
            carry = carry_scr[qi]
            mask = (far_col + k_start) < k_end
            z, sp = logits(q2, kblk, mask)
            neg_cum = _dot(sp, incl_far)
            acc_scr[qi] += _dot(weights(z, neg_cum, carry, mask), vblk)
            carry_scr[qi] = carry + neg_cum[:, 0:1]
            return k_end - TK, unfinished()

        lax.while_loop(cond, body, (io * TQ_OUTER + qi * TQ - TK, unfinished()))
        write_out(q_rows, acc_scr[qi])
        return 0

    @pl.when(worst > EXP2_ZERO_BOUND)
    def _():
        lax.fori_loop(0, n_q, finish, 0)


def _attention(q, k, v, cast_weights=()):
    B, S, W = q.shape
    n_pairs = W // LANES
    n_io = S // TQ_OUTER
    n_steps = B * n_pairs * n_io
    qspec = pl.BlockSpec((1, TQ_OUTER, LANES), lambda b, p, i: (b, i, p))
    kvspec = pl.BlockSpec((1, S, LANES), lambda b, p, i: (b, 0, p))
    step = lambda b, p, i: (b * n_pairs + p) * n_io + i
    w_in_specs, w_out_specs, w_out_shapes = [], [], []
    for w, layer in cast_weights:
        _, R, C = w.shape
        rows = R // n_steps
        assert rows * n_steps == R and rows % (2 * SUBLANES) == 0, (R, n_steps)
        w_in_specs.append(pl.BlockSpec(
            (None, rows, C), lambda b, p, i, layer=layer: (layer, step(b, p, i), 0)))
        w_out_specs.append(pl.BlockSpec((rows, C), lambda b, p, i: (step(b, p, i), 0)))
        w_out_shapes.append(jax.ShapeDtypeStruct((R, C), BF16))
    out = pl.pallas_call(
        functools.partial(_attention_body, len(cast_weights)),
        grid=(B, n_pairs, n_io),
        in_specs=[qspec, kvspec, kvspec] + w_in_specs,
        out_specs=[qspec] + w_out_specs,
        out_shape=[jax.ShapeDtypeStruct((B, S, W), BF16)] + w_out_shapes,
        scratch_shapes=[
            pltpu.VMEM((TQ_OUTER // TQ, 2 * TQ, LANES), F32),
            pltpu.VMEM((TQ_OUTER // TQ, 2 * TQ, 1), F32),
        ],
        compiler_params=pltpu.CompilerParams(
            dimension_semantics=("arbitrary", "arbitrary", "arbitrary"),
            vmem_limit_bytes=VMEM_LIMIT_BYTES),
        name="attention",
    )(q, k, v, *(w for w, _ in cast_weights))
    return out[0], tuple(out[1:])


def _merge_ffn_body(layer, x_ref, mab_ref, gc_ref, yc_ref, wc_ref, wo_ref, fg_ref, wgu_ref, wd_ref,
                    o_ref, act_scr):
    def stage_branch(st):
        st["ydc"] = _dot(yc_ref[st["rows"], :], wc_ref[...])

    def stage_wo(st):
        rows = st["rows"]
        merged = mab_ref[rows, :].astype(F32) + gc_ref[rows, :].astype(F32) * st["ydc"]
        x1 = x_ref[rows, :] + _dot(merged.astype(BF16), wo_ref[...])
        o_ref[rows, :] = x1
        ms = jnp.mean(x1 * x1, axis=-1, keepdims=True)
        st["h2"] = (x1 * lax.rsqrt(ms + EPS) * fg_ref[layer:layer + 1, :]).astype(BF16)

    def stage_gate_up(st, c0):
        c1 = min(c0 + FF_SPLIT, D_FF)
        gate = _dot(st["h2"], wgu_ref[:, c0:c1])
        up = _dot(st["h2"], wgu_ref[:, D_FF + c0:D_FF + c1])
        act_scr[st["rows"], c0:c1] = (gate * _sigmoid(gate) * up).astype(BF16)

    def stage_down(st):
        rows = st["rows"]
        o_ref[rows, :] += _dot(act_scr[rows, :], wd_ref[...])

    subs = [dict(rows=slice(r0, r0 + TS_FFN)) for r0 in range(0, x_ref.shape[0], TS_FFN)]
    for stage in (stage_branch, stage_wo):
        for st in subs:
            stage(st)
    for c0 in range(0, D_FF, FF_SPLIT):
        for st in subs:
            stage_gate_up(st, c0)
    for st in subs:
        stage_down(st)


def _merge_ffn(l, x, mab, gc, yc, wbo, wo, fg, wgu, wd):
    M, D = x.shape
    tm = TM_FFN
    tok = lambda width: pl.BlockSpec((tm, width), lambda i: (i, 0))
    return pl.pallas_call(
        functools.partial(_merge_ffn_body, l),
        grid=(M // tm,),
        in_specs=[
            tok(D), tok(D), tok(D), tok(ATT_WIDTH),
            pl.BlockSpec((None, CONV_CH, D), lambda *_: (2, 0, 0),
                         pipeline_mode=pl.Buffered(1)),
            _resident(wo.shape), _resident(fg.shape),
            _resident(wgu.shape), _resident(wd.shape),
        ],
        out_specs=tok(D),
        out_shape=jax.ShapeDtypeStruct((M, D), F32),
        scratch_shapes=[pltpu.VMEM((tm, D_FF), BF16)],
        compiler_params=pltpu.CompilerParams(
            dimension_semantics=("arbitrary",),
            vmem_limit_bytes=VMEM_LIMIT_BYTES),
        name="merge_ffn",
    )(x, mab, gc, yc, wbo, wo, fg, wgu, wd)


def kernel(x, mix_norm_g, w_in, b_gate, conv_w, conv_b, sgu_ln_g, sgu_ln_b, sgu_w, sgu_b,
           q_norm_g, k_norm_g, w_branch_out, w_o, ffn_norm_g, w_gate_up, w_down):
    B, S, D = x.shape
    depth = w_in.shape[0]
    assert D == D_MODEL and w_in.shape[1:] == (D_MODEL, IN_COLS), (x.shape, w_in.shape)
    assert conv_w.shape[1:] == (CONV_K, CONV_CH) and CONV_K == 3, conv_w.shape
    assert sgu_w.shape[1:] == (SGU_GROUPS, CHUNK, CHUNK), sgu_w.shape
    assert w_gate_up.shape[1:] == (D_MODEL, 2 * D_FF) and w_down.shape[1:] == (D_FF, D_MODEL)
    assert S % TQ_OUTER == 0 and S % TM_IN == 0 and (B * S) % TM_FFN == 0, (B, S)
    sbf = jnp.repeat(jnp.swapaxes(sgu_b, 1, 2), CHUNK, axis=2)
    qg = jnp.tile(q_norm_g, (1, LANES // HEAD_DIM))
    kg = jnp.tile(k_norm_g, (1, LANES // HEAD_DIM))
    w_bo = w_branch_out.reshape(depth, N_BRANCH * CONV_CH, D)
    win, wbo = _cast_bf16(w_in, 0), _cast_bf16(w_bo, 0)
    for l in range(depth):
        wbo = wbo.reshape(N_BRANCH, CONV_CH, D)
        mab, gc, q, k, v = _mixer_in(
            l, x, mix_norm_g, win, b_gate, conv_w, conv_b, sgu_ln_g, sgu_ln_b, sgu_w, sbf,
            qg, kg, wbo)
        casts = [(w_o, l), (w_gate_up, l), (w_down, l)]
        if l + 1 < depth:
            casts += [(w_in, l + 1), (w_bo, l + 1)]
        yc, cast = _attention(q, k, v, casts)
        wo, wgu, wd = cast[:3]
        x = _merge_ffn(
            l, x.reshape(B * S, D), mab.reshape(B * S, D), gc.reshape(B * S, D),
            yc.reshape(B * S, ATT_WIDTH), wbo, wo, ffn_norm_g, wgu, wd).reshape(B, S, D)
        if l + 1 < depth:
            win, wbo = cast[3:]
    return x
```

```python
import functools

import jax
import jax.numpy as jnp
from jax import lax
from jax.experimental import pallas as pl
from jax.experimental.pallas import tpu as pltpu

F32 = jnp.float32
BF16 = jnp.bfloat16

D_MODEL = 1024
CONV_CH = 512
CONV_K = 3
SGU_WIDTH = 512
SGU_GROUPS = 4
CHUNK = 128
ATT_HEADS = 8
HEAD_DIM = 64
ATT_WIDTH = ATT_HEADS * HEAD_DIM
N_BRANCH = 3
D_FF = 2816
IN_COLS = 7168
EPS = 1e-6

LANES = 128
SUBLANES = 8
VMEM_LIMIT_BYTES = 56 * 1024 * 1024

C_AB, C_AC, C_AX = 0, CONV_CH, 2 * CONV_CH
C_SU = 3 * CONV_CH
C_SV = C_SU + SGU_WIDTH
C_Q = C_SV + SGU_WIDTH
C_K, C_V = C_Q + ATT_WIDTH, C_Q + 2 * ATT_WIDTH
C_G = C_Q + 3 * ATT_WIDTH
assert C_G + N_BRANCH * D_MODEL == IN_COLS

TM_IN = 1024
TS_IN = 512
TM_FFN = 1024
TS_FFN = 512
TQ_OUTER = 4096
TQ = 128
TK = 256
MXU_DIM = 256
FF_SPLIT = 6 * MXU_DIM
CAST_BLOCK_BYTES = 8 * 1024 * 1024

LOG2E = 1.4426950408889634
EXP2_CLAMP = 126.0
STAGE_SKEW = 2

EXP2_ZERO_BOUND = -151.0


def _sigmoid(x):
    return 1.0 / (1.0 + jnp.exp(-x))


def _gelu_exact(x):
    return 0.5 * x * (1.0 + lax.erf(x * (2.0 ** -0.5)))


def _dot(a, b):
    return jnp.dot(a, b, preferred_element_type=F32)


def _resident(shape):
    zeros = (0,) * len(shape)
    return pl.BlockSpec(shape, lambda *_: zeros, pipeline_mode=pl.Buffered(1))


def _resident_layer(shape, layer):
    index = (layer,) + (0,) * (len(shape) - 1)
    return pl.BlockSpec((None,) + tuple(shape[1:]), lambda *_: index,
                        pipeline_mode=pl.Buffered(1))


def _cast_body(w_ref, o_ref):
    o_ref[...] = w_ref[...].astype(o_ref.dtype)


def _cast_rows(n_rows, n_cols):
    best = None
    for rows in range(2 * SUBLANES, n_rows + 1, 2 * SUBLANES):
        if n_rows % rows == 0 and rows * n_cols * 4 <= CAST_BLOCK_BYTES:
            best = rows
    assert best is not None, (n_rows, n_cols)
    return best


def _cast_bf16(w, layer):
    _, R, C = w.shape
    rows = _cast_rows(R, C)
    return pl.pallas_call(
        _cast_body,
        grid=(R // rows,),
        in_specs=[pl.BlockSpec((None, rows, C), lambda r: (layer, r, 0))],
        out_specs=pl.BlockSpec((rows, C), lambda r: (r, 0)),
        out_shape=jax.ShapeDtypeStruct((R, C), BF16),
        compiler_params=pltpu.CompilerParams(
            dimension_semantics=("arbitrary",),
            vmem_limit_bytes=VMEM_LIMIT_BYTES),
        name="cast_bf16",
    )(w)


def _head_rmsnorm(p, g128):
    tm = p.shape[0]
    lo = lax.broadcasted_iota(jnp.int32, (tm, LANES), 1) < HEAD_DIM
    outs = []
    for c in range(ATT_WIDTH // LANES):
        pc = p[:, c * LANES:(c + 1) * LANES]
        pp = pc * pc
        s_lo = jnp.sum(jnp.where(lo, pp, 0.0), axis=-1, keepdims=True)
        s_hi = jnp.sum(jnp.where(lo, 0.0, pp), axis=-1, keepdims=True)
        ms = jnp.where(lo, s_lo, s_hi) * (1.0 / HEAD_DIM)
        outs.append(pc * lax.rsqrt(ms + EPS) * g128)
    return jnp.concatenate(outs, axis=-1)


def _mixer_in_body(layer, x_ref, ng_ref, win_ref, bg_ref, cw_ref, cb_ref, lng_ref, lnb_ref,
                   sw_ref, sb_ref, qg_ref, kg_ref, wab_ref,
                   mab_ref, gc_ref, q_ref, k_ref, v_ref,
                   u_scr, yb_scr):
    tm = x_ref.shape[1]
    ts = TS_IN
    j = pl.program_id(1)
    lrow = slice(layer, layer + 1)

    @pl.when(j == 0)
    def _():
        u_scr[0:SUBLANES, :] = jnp.zeros((SUBLANES, CONV_CH), F32)

    row = lax.broadcasted_iota(jnp.int32, (CHUNK, CHUNK), 0)
    col = lax.broadcasted_iota(jnp.int32, (CHUNK, CHUNK), 1)
    sgu_w = [jnp.where(col <= row, sw_ref[g], 0.0).astype(BF16) for g in range(SGU_GROUPS)]

    def proj(st, c0, width):
        return _dot(st["h"], win_ref[:, c0:c0 + width])

    def stage_norm(st):
        x = x_ref[0, st["rows"], :]
        ms = jnp.mean(x * x, axis=-1, keepdims=True)
        st["h"] = (x * lax.rsqrt(ms + EPS) * ng_ref[lrow, :]).astype(BF16)

    def stage_conv(st):
        u = proj(st, C_AC, CONV_CH) * proj(st, C_AX, CONV_CH)
        u_scr[SUBLANES:SUBLANES + ts, :] = u
        u1 = u_scr[SUBLANES - 1:SUBLANES - 1 + ts, :]
        u2 = u_scr[SUBLANES - 2:SUBLANES - 2 + ts, :]
        y = cb_ref[lrow, :] + cw_ref[0:1, :] * u2
        y = y + cw_ref[1:2, :] * u1
        y = y + cw_ref[2:3, :] * u
        st["ya"] = (proj(st, C_AB, CONV_CH) * y).astype(BF16)
        u_scr[0:SUBLANES, :] = u_scr[ts:ts + SUBLANES, :]

    def stage_sgu_in(st):
        st["gu"] = _gelu_exact(proj(st, C_SU, SGU_WIDTH))
        gv = _gelu_exact(proj(st, C_SV, SGU_WIDTH))
        mu = jnp.mean(gv, axis=-1, keepdims=True)
        vc = gv - mu
        var = jnp.mean(vc * vc, axis=-1, keepdims=True)
        st["vn"] = (vc * lax.rsqrt(var + EPS) * lng_ref[lrow, :] + lnb_ref[lrow, :]).astype(BF16)

    def stage_qkv(st):
        rows = st["rows"]
        q = _head_rmsnorm(proj(st, C_Q, ATT_WIDTH), qg_ref[lrow, :]) * (HEAD_DIM ** -0.5 * LOG2E)
        q_ref[0, rows, :] = q.astype(BF16)
        k_ref[0, rows, :] = _head_rmsnorm(proj(st, C_K, ATT_WIDTH), kg_ref[lrow, :]).astype(BF16)
        v_ref[0, rows, :] = proj(st, C_V, ATT_WIDTH).astype(BF16)

    def stage_sgu_mix(st):
        r0 = st["rows"].start
        for g in range(SGU_GROUPS):
            cs = slice(g * CHUNK, (g + 1) * CHUNK)
            for c in range(ts // CHUNK):
                rs = slice(c * CHUNK, (c + 1) * CHUNK)
                mixed = _dot(sgu_w[g], st["vn"][rs, cs]) + sb_ref[:, cs]
                yb_scr[r0 + c * CHUNK:r0 + (c + 1) * CHUNK, cs] = (
                    st["gu"][rs, cs] * mixed).astype(BF16)

    def stage_gates(st):
        rows = st["rows"]
        ga = _sigmoid(proj(st, C_G, D_MODEL) + bg_ref[lrow, 0:D_MODEL])
        gb = _sigmoid(proj(st, C_G + D_MODEL, D_MODEL) + bg_ref[lrow, D_MODEL:2 * D_MODEL])
        gc = _sigmoid(proj(st, C_G + 2 * D_MODEL, D_MODEL) + bg_ref[lrow, 2 * D_MODEL:3 * D_MODEL])
        gc_ref[0, rows, :] = gc.astype(BF16)
        mab = ga * _dot(st["ya"], wab_ref[0]) + gb * _dot(yb_scr[rows, :], wab_ref[1])
        mab_ref[0, rows, :] = mab.astype(BF16)

    subs = [dict(rows=slice(r0, r0 + ts)) for r0 in range(0, tm, ts)]
    for st in subs:
        stage_norm(st)
    for i in range(len(subs) + 1):
        if i < len(subs):
            stage_conv(subs[i])
            stage_sgu_in(subs[i])
            stage_qkv(subs[i])
        if i >= 1:
            stage_sgu_mix(subs[i - 1])
            stage_gates(subs[i - 1])


def _mixer_in(l, x, ng, win, bg, cw, cb, lng, lnb, sw, sbf, qg, kg, wbo):
    B, S, D = x.shape
    tm = TM_IN
    tok = lambda width: pl.BlockSpec((1, tm, width), lambda b, j: (b, j, 0))
    out_shape = (
        jax.ShapeDtypeStruct((B, S, D), BF16),
        jax.ShapeDtypeStruct((B, S, D), BF16),
        jax.ShapeDtypeStruct((B, S, ATT_WIDTH), BF16),
        jax.ShapeDtypeStruct((B, S, ATT_WIDTH), BF16),
        jax.ShapeDtypeStruct((B, S, ATT_WIDTH), BF16),
    )
    return pl.pallas_call(
        functools.partial(_mixer_in_body, l),
        grid=(B, S // tm),
        in_specs=[
            tok(D),
            _resident(ng.shape), _resident(win.shape), _resident(bg.shape),
            _resident_layer(cw.shape, l), _resident(cb.shape),
            _resident(lng.shape), _resident(lnb.shape),
            _resident_layer(sw.shape, l), _resident_layer(sbf.shape, l),
            _resident(qg.shape), _resident(kg.shape),
            _resident((2, CONV_CH, D)),
        ],
        out_specs=(tok(D), tok(D), tok(ATT_WIDTH), tok(ATT_WIDTH), tok(ATT_WIDTH)),
        out_shape=out_shape,
        scratch_shapes=[
            pltpu.VMEM((TS_IN + SUBLANES, CONV_CH), F32),
            pltpu.VMEM((tm, SGU_WIDTH), BF16),
        ],
        compiler_params=pltpu.CompilerParams(
            dimension_semantics=("arbitrary", "arbitrary"),
            vmem_limit_bytes=VMEM_LIMIT_BYTES),
        name="mixer_in",
    )(x, ng, win, bg, cw, cb, lng, lnb, sw, sbf, qg, kg, wbo)


def _attention_body(n_cast, q_ref, k_ref, v_ref, *refs):
    w_refs, o_ref = refs[:n_cast], refs[n_cast]
    wo_refs = refs[n_cast + 1:2 * n_cast + 1]
    acc_scr, carry_scr = refs[2 * n_cast + 1:]
    for w_ref, wo_ref in zip(w_refs, wo_refs):
        wo_ref[...] = w_ref[...].astype(wo_ref.dtype)

    io = pl.program_id(2)
    n_q = TQ_OUTER // TQ
    lo = lax.broadcasted_iota(jnp.int32, (TQ, LANES), 1) < HEAD_DIM

    def neg_exclusive(n):
        return jnp.where(lax.broadcasted_iota(jnp.int32, (n, n), 0)
                         > lax.broadcasted_iota(jnp.int32, (n, n), 1), -1.0, 0.0).astype(BF16)

    cum_diag = jnp.concatenate([neg_exclusive(TQ), jnp.full((TQ, LANES), -1.0, BF16)], axis=1)
    cum_far = neg_exclusive(TK)
    diag_mask = (lax.broadcasted_iota(jnp.int32, (2 * TQ, TQ), 1)
                 < lax.broadcasted_iota(jnp.int32, (2 * TQ, TQ), 0) % TQ)
    far_col = lax.broadcasted_iota(jnp.int32, (2 * TQ, TK), 1)

    def stack_heads(q):
        zero = jnp.zeros_like(q)
        return jnp.concatenate([jnp.where(lo, q, zero), jnp.where(lo, zero, q)], axis=0)

    def logits(q2, kblk, mask):
        z = lax.dot_general(q2, kblk, (((1,), (1,)), ((), ())), preferred_element_type=F32)
        sp = jnp.maximum(z, jnp.log(1.0 + jnp.exp2(jnp.minimum(z, EXP2_CLAMP))) * LOG2E)
        log_beta = z - sp
        if mask is not None:
            sp = jnp.where(mask, sp, 0.0)
        return log_beta, sp.astype(BF16)

    def weights(log_beta, neg_cum, carry, mask):
        e = log_beta + neg_cum
        if carry is not None:
            e = e + carry
        a = jnp.exp2(e)
        if mask is not None:
            a = jnp.where(mask, a, 0.0)
        return a.astype(BF16)

    def write_out(q_rows, acc):
        o_ref[0, q_rows, :] = jnp.where(lo, acc[:TQ], acc[TQ:]).astype(o_ref.dtype)

    def stage_logits(qi):
        q_start = pl.multiple_of(io * TQ_OUTER + qi * TQ, TQ)
        far_start = pl.multiple_of(jnp.maximum(q_start - TK, 0), TQ)
        q2 = stack_heads(q_ref[0, qi * TQ:(qi + 1) * TQ, :])
        far_mask = (far_col + far_start) < q_start if qi * TQ < TK else None
        lb_d, sp_d = logits(q2, k_ref[0, pl.ds(q_start, TQ), :], diag_mask)
        lb_f, sp_f = logits(q2, k_ref[0, pl.ds(far_start, TK), :], far_mask)
        return dict(q_start=q_start, far_start=far_start, far_mask=far_mask,
                    lb_d=lb_d, sp_d=sp_d, lb_f=lb_f, sp_f=sp_f)

    def stage_weights(st):
        neg_d = _dot(st["sp_d"], cum_diag)
        neg_f = _dot(st["sp_f"], cum_far)
        carry_d = neg_d[:, TQ:]
        a_d = weights(st["lb_d"], neg_d[:, :TQ], None, diag_mask)
        a_f = weights(st["lb_f"], neg_f, jnp.concatenate([carry_d] * (TK // LANES), axis=1),
                      st["far_mask"])
        return dict(q_start=st["q_start"], far_start=st["far_start"],
                    a_d=a_d, a_f=a_f,
                    carry=carry_d[:, 0:1] + neg_f[:, 0:1] - st["sp_f"][:, 0:1].astype(F32))

    def stage_values(st):
        pv_d = _dot(st["a_d"], v_ref[0, pl.ds(st["q_start"], TQ), :])
        pv_f = _dot(st["a_f"], v_ref[0, pl.ds(st["far_start"], TK), :])
        return pv_d + pv_f, st["carry"]

    after_logits, after_weights, accs, carries = {}, {}, [], []
    for t in range(n_q + 2 * STAGE_SKEW):
        if t < n_q:
            after_logits[t] = stage_logits(t)
        if 0 <= t - STAGE_SKEW < n_q:
            after_weights[t - STAGE_SKEW] = stage_weights(after_logits.pop(t - STAGE_SKEW))
        if 0 <= t - 2 * STAGE_SKEW < n_q:
            acc, carry = stage_values(after_weights.pop(t - 2 * STAGE_SKEW))
            accs.append(acc)
            carries.append(carry)

    n_first = TK // TQ + 1
    worst = jnp.max(functools.reduce(jnp.maximum, carries[n_first:]))
    worst_first = jnp.max(functools.reduce(jnp.maximum, carries[:n_first]))
    worst = jnp.where(io > 0, jnp.maximum(worst, worst_first), worst)
    for qi in range(n_q):
        write_out(slice(qi * TQ, (qi + 1) * TQ), accs[qi])
        acc_scr[qi] = accs[qi]
        carry_scr[qi] = carries[qi]

    def finish(qi, _):
        q_rows = pl.ds(pl.multiple_of(qi * TQ, TQ), TQ)
        q2 = stack_heads(q_ref[0, q_rows, :])

        def unfinished():
            return jnp.max(carry_scr[qi]) > EXP2_ZERO_BOUND

        def cond(state):
            k_end, go = state
            return jnp.logical_and(k_end > 0, go)

        def body(state):
            k_end, _ = state
            k_start = pl.multiple_of(jnp.maximum(k_end - TK, 0), TQ)
            kblk = k_ref[0, pl.ds(k_start, TK), :]
            vblk = v_ref[0, pl.ds(k_start, TK), :]
            carry = carry_scr[qi]
            mask = (far_col + k_start) < k_end
            log_beta, sp = logits(q2, kblk, mask)
            neg_cum = _dot(sp, cum_far)
            acc_scr[qi] += _dot(weights(log_beta, neg_cum, carry, mask), vblk)
            carry_scr[qi] = carry + neg_cum[:, 0:1] - sp[:, 0:1].astype(F32)
            return k_end - TK, unfinished()

        lax.while_loop(cond, body, (io * TQ_OUTER + qi * TQ - TK, unfinished()))
        write_out(q_rows, acc_scr[qi])
        return 0

    @pl.when(worst > EXP2_ZERO_BOUND)
    def _():
        lax.fori_loop(0, n_q, finish, 0)


def _attention(q, k, v, cast_weights=()):
    B, S, W = q.shape
    n_pairs = W // LANES
    n_io = S // TQ_OUTER
    n_steps = B * n_pairs * n_io
    qspec = pl.BlockSpec((1, TQ_OUTER, LANES), lambda b, p, i: (b, i, p))
    kvspec = pl.BlockSpec((1, S, LANES), lambda b, p, i: (b, 0, p))
    step = lambda b, p, i: (b * n_pairs + p) * n_io + i
    w_in_specs, w_out_specs, w_out_shapes = [], [], []
    for w, layer in cast_weights:
        _, R, C = w.shape
        rows = R // n_steps
        assert rows * n_steps == R and rows % (2 * SUBLANES) == 0, (R, n_steps)
        w_in_specs.append(pl.BlockSpec(
            (None, rows, C), lambda b, p, i, layer=layer: (layer, step(b, p, i), 0)))
        w_out_specs.append(pl.BlockSpec((rows, C), lambda b, p, i: (step(b, p, i), 0)))
        w_out_shapes.append(jax.ShapeDtypeStruct((R, C), BF16))
    out = pl.pallas_call(
        functools.partial(_attention_body, len(cast_weights)),
        grid=(B, n_pairs, n_io),
        in_specs=[qspec, kvspec, kvspec] + w_in_specs,
        out_specs=[qspec] + w_out_specs,
        out_shape=[jax.ShapeDtypeStruct((B, S, W), BF16)] + w_out_shapes,
        scratch_shapes=[
            pltpu.VMEM((TQ_OUTER // TQ, 2 * TQ, LANES), F32),
            pltpu.VMEM((TQ_OUTER // TQ, 2 * TQ, 1), F32),
        ],
        compiler_params=pltpu.CompilerParams(
            dimension_semantics=("arbitrary", "arbitrary", "arbitrary"),
            vmem_limit_bytes=VMEM_LIMIT_BYTES),
        name="attention",
    )(q, k, v, *(w for w, _ in cast_weights))
    return out[0], tuple(out[1:])


def _merge_ffn_body(layer, x_ref, mab_ref, gc_ref, yc_ref, wc_ref, wo_ref, fg_ref, wgu_ref, wd_ref,
                    o_ref, act_scr):
    def stage_branch(st):
        st["ydc"] = _dot(yc_ref[st["rows"], :], wc_ref[...])

    def stage_wo(st):
        rows = st["rows"]
        merged = mab_ref[rows, :].astype(F32) + gc_ref[rows, :].astype(F32) * st["ydc"]
        x1 = x_ref[rows, :] + _dot(merged.astype(BF16), wo_ref[...])
        o_ref[rows, :] = x1
        ms = jnp.mean(x1 * x1, axis=-1, keepdims=True)
        st["h2"] = (x1 * lax.rsqrt(ms + EPS) * fg_ref[layer:layer + 1, :]).astype(BF16)

    def stage_gate_up(st, c0):
        c1 = min(c0 + FF_SPLIT, D_FF)
        gate = _dot(st["h2"], wgu_ref[:, c0:c1])
        up = _dot(st["h2"], wgu_ref[:, D_FF + c0:D_FF + c1])
        act_scr[st["rows"], c0:c1] = (gate * _sigmoid(gate) * up).astype(BF16)

    def stage_down(st):
        rows = st["rows"]
        o_ref[rows, :] += _dot(act_scr[rows, :], wd_ref[...])

    subs = [dict(rows=slice(r0, r0 + TS_FFN)) for r0 in range(0, x_ref.shape[0], TS_FFN)]
    for stage in (stage_branch, stage_wo):
        for st in subs:
            stage(st)
    for c0 in range(0, D_FF, FF_SPLIT):
        for st in subs:
            stage_gate_up(st, c0)
    for st in subs:
        stage_down(st)


def _merge_ffn(l, x, mab, gc, yc, wbo, wo, fg, wgu, wd):
    M, D = x.shape
    tm = TM_FFN
    tok = lambda width: pl.BlockSpec((tm, width), lambda i: (i, 0))
    return pl.pallas_call(
        functools.partial(_merge_ffn_body, l),
        grid=(M // tm,),
        in_specs=[
            tok(D), tok(D), tok(D), tok(ATT_WIDTH),
            pl.BlockSpec((None, CONV_CH, D), lambda *_: (2, 0, 0),
                         pipeline_mode=pl.Buffered(1)),
            _resident(wo.shape), _resident(fg.shape),
            _resident(wgu.shape), _resident(wd.shape),
        ],
        out_specs=tok(D),
        out_shape=jax.ShapeDtypeStruct((M, D), F32),
        scratch_shapes=[pltpu.VMEM((tm, D_FF), BF16)],
        compiler_params=pltpu.CompilerParams(
            dimension_semantics=("arbitrary",),
            vmem_limit_bytes=VMEM_LIMIT_BYTES),
        name="merge_ffn",
    )(x, mab, gc, yc, wbo, wo, fg, wgu, wd)


def kernel(x, mix_norm_g, w_in, b_gate, conv_w, conv_b, sgu_ln_g, sgu_ln_b, sgu_w, sgu_b,
           q_norm_g, k_norm_g, w_branch_out, w_o, ffn_norm_g, w_gate_up, w_down):
    B, S, D = x.shape
    depth = w_in.shape[0]
    assert D == D_MODEL and w_in.shape[1:] == (D_MODEL, IN_COLS), (x.shape, w_in.shape)
    assert conv_w.shape[1:] == (CONV_K, CONV_CH) and CONV_K == 3, conv_w.shape
    assert sgu_w.shape[1:] == (SGU_GROUPS, CHUNK, CHUNK), sgu_w.shape
    assert w_gate_up.shape[1:] == (D_MODEL, 2 * D_FF) and w_down.shape[1:] == (D_FF, D_MODEL)
    assert S % TQ_OUTER == 0 and S % TM_IN == 0 and (B * S) % TM_FFN == 0, (B, S)
    sbf = jnp.repeat(jnp.swapaxes(sgu_b, 1, 2), CHUNK, axis=2)
    qg = jnp.tile(q_norm_g, (1, LANES // HEAD_DIM))
    kg = jnp.tile(k_norm_g, (1, LANES // HEAD_DIM))
    w_bo = w_branch_out.reshape(depth, N_BRANCH * CONV_CH, D)
    win, wbo = _cast_bf16(w_in, 0), _cast_bf16(w_bo, 0)
    for l in range(depth):
        wbo = wbo.reshape(N_BRANCH, CONV_CH, D)
        mab, gc, q, k, v = _mixer_in(
            l, x, mix_norm_g, win, b_gate, conv_w, conv_b, sgu_ln_g, sgu_ln_b, sgu_w, sbf,
            qg, kg, wbo)
        casts = [(w_o, l), (w_gate_up, l), (w_down, l)]
        if l + 1 < depth:
            casts += [(w_in, l + 1), (w_bo, l + 1)]
        yc, cast = _attention(q, k, v, casts)
        wo, wgu, wd = cast[:3]
        x = _merge_ffn(
            l, x.reshape(B * S, D), mab.reshape(B * S, D), gc.reshape(B * S, D),
            yc.reshape(B * S, ATT_WIDTH), wbo, wo, ffn_norm_g, wgu, wd).reshape(B, S, D)
        if l + 1 < depth:
            win, wbo = cast[3:]
    return x
```

```python
import functools

import jax
import jax.numpy as jnp
from jax import lax
from jax.experimental import pallas as pl
from jax.experimental.pallas import tpu as pltpu

F32 = jnp.float32
BF16 = jnp.bfloat16

D_MODEL = 1024
CONV_CH = 512
CONV_K = 3
SGU_WIDTH = 512
SGU_GROUPS = 4
CHUNK = 128
ATT_HEADS = 8
HEAD_DIM = 64
ATT_WIDTH = ATT_HEADS * HEAD_DIM
N_BRANCH = 3
D_FF = 2816
IN_COLS = 7168
EPS = 1e-6

LANES = 128
SUBLANES = 8
VMEM_LIMIT_BYTES = 56 * 1024 * 1024

C_AB, C_AC, C_AX = 0, CONV_CH, 2 * CONV_CH
C_SU = 3 * CONV_CH
C_SV = C_SU + SGU_WIDTH
C_Q = C_SV + SGU_WIDTH
C_K, C_V = C_Q + ATT_WIDTH, C_Q + 2 * ATT_WIDTH
C_G = C_Q + 3 * ATT_WIDTH
assert C_G + N_BRANCH * D_MODEL == IN_COLS

TM_IN = 1024
TS_IN = 512
TM_FFN = 1024
TS_FFN = 512
TQ_OUTER = 4096
TQ = 128
TK = 256
MXU_DIM = 256
FF_SPLIT = 6 * MXU_DIM
CAST_BLOCK_BYTES = 8 * 1024 * 1024

LOG2E = 1.4426950408889634
EXP2_CLAMP = 126.0
STAGE_SKEW = 1

EXP2_ZERO_BOUND = -151.0


def _sigmoid(x):
    return 1.0 / (1.0 + jnp.exp(-x))


def _gelu_exact(x):
    return 0.5 * x * (1.0 + lax.erf(x * (2.0 ** -0.5)))


def _dot(a, b):
    return jnp.dot(a, b, preferred_element_type=F32)


def _resident(shape):
    zeros = (0,) * len(shape)
    return pl.BlockSpec(shape, lambda *_: zeros, pipeline_mode=pl.Buffered(1))


def _resident_layer(shape, layer):
    index = (layer,) + (0,) * (len(shape) - 1)
    return pl.BlockSpec((None,) + tuple(shape[1:]), lambda *_: index,
                        pipeline_mode=pl.Buffered(1))


def _cast_body(w_ref, o_ref):
    o_ref[...] = w_ref[...].astype(o_ref.dtype)


def _cast_rows(n_rows, n_cols):
    best = None
    for rows in range(2 * SUBLANES, n_rows + 1, 2 * SUBLANES):
        if n_rows % rows == 0 and rows * n_cols * 4 <= CAST_BLOCK_BYTES:
            best = rows
    assert best is not None, (n_rows, n_cols)
    return best


def _cast_bf16(w, layer):
    _, R, C = w.shape
    rows = _cast_rows(R, C)
    return pl.pallas_call(
        _cast_body,
        grid=(R // rows,),
        in_specs=[pl.BlockSpec((None, rows, C), lambda r: (layer, r, 0))],
        out_specs=pl.BlockSpec((rows, C), lambda r: (r, 0)),
        out_shape=jax.ShapeDtypeStruct((R, C), BF16),
        compiler_params=pltpu.CompilerParams(
            dimension_semantics=("arbitrary",),
            vmem_limit_bytes=VMEM_LIMIT_BYTES),
        name="cast_bf16",
    )(w)


def _head_rmsnorm(p, g128):
    tm = p.shape[0]
    lo = lax.broadcasted_iota(jnp.int32, (tm, LANES), 1) < HEAD_DIM
    outs = []
    for c in range(ATT_WIDTH // LANES):
        pc = p[:, c * LANES:(c + 1) * LANES]
        pp = pc * pc
        s_lo = jnp.sum(jnp.where(lo, pp, 0.0), axis=-1, keepdims=True)
        s_hi = jnp.sum(jnp.where(lo, 0.0, pp), axis=-1, keepdims=True)
        ms = jnp.where(lo, s_lo, s_hi) * (1.0 / HEAD_DIM)
        outs.append(pc * lax.rsqrt(ms + EPS) * g128)
    return jnp.concatenate(outs, axis=-1)


def _mixer_in_body(layer, x_ref, ng_ref, win_ref, bg_ref, cw_ref, cb_ref, lng_ref, lnb_ref,
                   sw_ref, sb_ref, qg_ref, kg_ref, wab_ref,
                   mab_ref, gc_ref, q_ref, k_ref, v_ref,
                   u_scr, yb_scr):
    tm = x_ref.shape[1]
    ts = TS_IN
    j = pl.program_id(1)
    lrow = slice(layer, layer + 1)

    @pl.when(j == 0)
    def _():
        u_scr[0:SUBLANES, :] = jnp.zeros((SUBLANES, CONV_CH), F32)

    row = lax.broadcasted_iota(jnp.int32, (CHUNK, CHUNK), 0)
    col = lax.broadcasted_iota(jnp.int32, (CHUNK, CHUNK), 1)
    sgu_w = [jnp.where(col <= row, sw_ref[g], 0.0).astype(BF16) for g in range(SGU_GROUPS)]

    def proj(st, c0, width):
        return _dot(st["h"], win_ref[:, c0:c0 + width])

    def stage_norm(st):
        x = x_ref[0, st["rows"], :]
        ms = jnp.mean(x * x, axis=-1, keepdims=True)
        st["h"] = (x * lax.rsqrt(ms + EPS) * ng_ref[lrow, :]).astype(BF16)

    def stage_conv(st):
        u = proj(st, C_AC, CONV_CH) * proj(st, C_AX, CONV_CH)
        u_scr[SUBLANES:SUBLANES + ts, :] = u
        u1 = u_scr[SUBLANES - 1:SUBLANES - 1 + ts, :]
        u2 = u_scr[SUBLANES - 2:SUBLANES - 2 + ts, :]
        y = cb_ref[lrow, :] + cw_ref[0:1, :] * u2
        y = y + cw_ref[1:2, :] * u1
        y = y + cw_ref[2:3, :] * u
        st["ya"] = (proj(st, C_AB, CONV_CH) * y).astype(BF16)
        u_scr[0:SUBLANES, :] = u_scr[ts:ts + SUBLANES, :]

    def stage_sgu_in(st):
        st["gu"] = _gelu_exact(proj(st, C_SU, SGU_WIDTH))
        gv = _gelu_exact(proj(st, C_SV, SGU_WIDTH))
        mu = jnp.mean(gv, axis=-1, keepdims=True)
        vc = gv - mu
        var = jnp.mean(vc * vc, axis=-1, keepdims=True)
        st["vn"] = (vc * lax.rsqrt(var + EPS) * lng_ref[lrow, :] + lnb_ref[lrow, :]).astype(BF16)

    def stage_qkv(st):
        rows = st["rows"]
        q = _head_rmsnorm(proj(st, C_Q, ATT_WIDTH), qg_ref[lrow, :]) * (HEAD_DIM ** -0.5 * LOG2E)
        q_ref[0, rows, :] = q.astype(BF16)
        k_ref[0, rows, :] = _head_rmsnorm(proj(st, C_K, ATT_WIDTH), kg_ref[lrow, :]).astype(BF16)
        v_ref[0, rows, :] = proj(st, C_V, ATT_WIDTH).astype(BF16)

    def stage_sgu_mix(st):
        r0 = st["rows"].start
        for g in range(SGU_GROUPS):
            cs = slice(g * CHUNK, (g + 1) * CHUNK)
            for c in range(ts // CHUNK):
                rs = slice(c * CHUNK, (c + 1) * CHUNK)
                mixed = _dot(sgu_w[g], st["vn"][rs, cs]) + sb_ref[:, cs]
                yb_scr[r0 + c * CHUNK:r0 + (c + 1) * CHUNK, cs] = (
                    st["gu"][rs, cs] * mixed).astype(BF16)

    def stage_gates(st):
        rows = st["rows"]
        ga = _sigmoid(proj(st, C_G, D_MODEL) + bg_ref[lrow, 0:D_MODEL])
        gb = _sigmoid(proj(st, C_G + D_MODEL, D_MODEL) + bg_ref[lrow, D_MODEL:2 * D_MODEL])
        gc = _sigmoid(proj(st, C_G + 2 * D_MODEL, D_MODEL) + bg_ref[lrow, 2 * D_MODEL:3 * D_MODEL])
        gc_ref[0, rows, :] = gc.astype(BF16)
        mab = ga * _dot(st["ya"], wab_ref[0]) + gb * _dot(yb_scr[rows, :], wab_ref[1])
        mab_ref[0, rows, :] = mab.astype(BF16)

    subs = [dict(rows=slice(r0, r0 + ts)) for r0 in range(0, tm, ts)]
    for st in subs:
        stage_norm(st)
    for i in range(len(subs) + 1):
        if i < len(subs):
            stage_conv(subs[i])
            stage_sgu_in(subs[i])
            stage_qkv(subs[i])
        if i >= 1:
            stage_sgu_mix(subs[i - 1])
            stage_gates(subs[i - 1])


def _mixer_in(l, x, ng, win, bg, cw, cb, lng, lnb, sw, sbf, qg, kg, wbo):
    B, S, D = x.shape
    tm = TM_IN
    tok = lambda width: pl.BlockSpec((1, tm, width), lambda b, j: (b, j, 0))
    out_shape = (
        jax.ShapeDtypeStruct((B, S, D), BF16),
        jax.ShapeDtypeStruct((B, S, D), BF16),
        jax.ShapeDtypeStruct((B, S, ATT_WIDTH), BF16),
        jax.ShapeDtypeStruct((B, S, ATT_WIDTH), BF16),
        jax.ShapeDtypeStruct((B, S, ATT_WIDTH), BF16),
    )
    return pl.pallas_call(
        functools.partial(_mixer_in_body, l),
        grid=(B, S // tm),
        in_specs=[
            tok(D),
            _resident(ng.shape), _resident(win.shape), _resident(bg.shape),
            _resident_layer(cw.shape, l), _resident(cb.shape),
            _resident(lng.shape), _resident(lnb.shape),
            _resident_layer(sw.shape, l), _resident_layer(sbf.shape, l),
            _resident(qg.shape), _resident(kg.shape),
            _resident((2, CONV_CH, D)),
        ],
        out_specs=(tok(D), tok(D), tok(ATT_WIDTH), tok(ATT_WIDTH), tok(ATT_WIDTH)),
        out_shape=out_shape,
        scratch_shapes=[
            pltpu.VMEM((TS_IN + SUBLANES, CONV_CH), F32),
            pltpu.VMEM((tm, SGU_WIDTH), BF16),
        ],
        compiler_params=pltpu.CompilerParams(
            dimension_semantics=("arbitrary", "arbitrary"),
            vmem_limit_bytes=VMEM_LIMIT_BYTES),
        name="mixer_in",
    )(x, ng, win, bg, cw, cb, lng, lnb, sw, sbf, qg, kg, wbo)


def _attention_body(n_cast, q_ref, k_ref, v_ref, *refs):
    w_refs, o_ref = refs[:n_cast], refs[n_cast]
    wo_refs = refs[n_cast + 1:2 * n_cast + 1]
    acc_scr, carry_scr = refs[2 * n_cast + 1:]
    for w_ref, wo_ref in zip(w_refs, wo_refs):
        wo_ref[...] = w_ref[...].astype(wo_ref.dtype)

    io = pl.program_id(2)
    n_q = TQ_OUTER // TQ
    lo = lax.broadcasted_iota(jnp.int32, (TQ, LANES), 1) < HEAD_DIM

    def neg_exclusive(n):
        return jnp.where(lax.broadcasted_iota(jnp.int32, (n, n), 0)
                         > lax.broadcasted_iota(jnp.int32, (n, n), 1), -1.0, 0.0).astype(BF16)

    cum_diag = jnp.concatenate([neg_exclusive(TQ), jnp.full((TQ, LANES), -1.0, BF16)], axis=1)
    cum_far = neg_exclusive(TK)
    diag_mask = (lax.broadcasted_iota(jnp.int32, (2 * TQ, TQ), 1)
                 < lax.broadcasted_iota(jnp.int32, (2 * TQ, TQ), 0) % TQ)
    far_col = lax.broadcasted_iota(jnp.int32, (2 * TQ, TK), 1)

    def stack_heads(q):
        zero = jnp.zeros_like(q)
        return jnp.concatenate([jnp.where(lo, q, zero), jnp.where(lo, zero, q)], axis=0)

    def logits(q2, kblk, mask):
        z = lax.dot_general(q2, kblk, (((1,), (1,)), ((), ())), preferred_element_type=F32)
        sp = jnp.maximum(z, jnp.log(1.0 + jnp.exp2(jnp.minimum(z, EXP2_CLAMP))) * LOG2E)
        log_beta = z - sp
        if mask is not None:
            sp = jnp.where(mask, sp, 0.0)
        return log_beta, sp.astype(BF16)

    def weights(log_beta, neg_cum, carry, mask):
        e = log_beta + neg_cum
        if carry is not None:
            e = e + carry
        a = jnp.exp2(e)
        if mask is not None:
            a = jnp.where(mask, a, 0.0)
        return a.astype(BF16)

    def write_out(q_rows, acc):
        o_ref[0, q_rows, :] = jnp.where(lo, acc[:TQ], acc[TQ:]).astype(o_ref.dtype)

    def stage_logits(qi):
        q_start = pl.multiple_of(io * TQ_OUTER + qi * TQ, TQ)
        far_start = pl.multiple_of(jnp.maximum(q_start - TK, 0), TQ)
        q2 = stack_heads(q_ref[0, qi * TQ:(qi + 1) * TQ, :])
        far_mask = (far_col + far_start) < q_start if qi * TQ < TK else None
        lb_d, sp_d = logits(q2, k_ref[0, pl.ds(q_start, TQ), :], diag_mask)
        lb_f, sp_f = logits(q2, k_ref[0, pl.ds(far_start, TK), :], far_mask)
        return dict(q_start=q_start, far_start=far_start, far_mask=far_mask,
                    lb_d=lb_d, sp_d=sp_d, lb_f=lb_f, sp_f=sp_f)

    def stage_weights(st):
        neg_d = _dot(st["sp_d"], cum_diag)
        neg_f = _dot(st["sp_f"], cum_far)
        carry_d = neg_d[:, TQ:]
        a_d = weights(st["lb_d"], neg_d[:, :TQ], None, diag_mask)
        a_f = weights(st["lb_f"], neg_f, jnp.concatenate([carry_d] * (TK // LANES), axis=1),
                      st["far_mask"])
        return dict(q_start=st["q_start"], far_start=st["far_start"],
                    a_d=a_d, a_f=a_f,
                    carry=carry_d[:, 0:1] + neg_f[:, 0:1] - st["sp_f"][:, 0:1].astype(F32))

    def stage_values(st):
        pv_d = _dot(st["a_d"], v_ref[0, pl.ds(st["q_start"], TQ), :])
        pv_f = _dot(st["a_f"], v_ref[0, pl.ds(st["far_start"], TK), :])
        return pv_d + pv_f, st["carry"]

    after_logits, after_weights, accs, carries = {}, {}, [], []
    for t in range(n_q + 2 * STAGE_SKEW):
        if t < n_q:
            after_logits[t] = stage_logits(t)
        if 0 <= t - STAGE_SKEW < n_q:
            after_weights[t - STAGE_SKEW] = stage_weights(after_logits.pop(t - STAGE_SKEW))
        if 0 <= t - 2 * STAGE_SKEW < n_q:
            acc, carry = stage_values(after_weights.pop(t - 2 * STAGE_SKEW))
            accs.append(acc)
            carries.append(carry)

    n_first = TK // TQ + 1
    worst = jnp.max(functools.reduce(jnp.maximum, carries[n_first:]))
    worst_first = jnp.max(functools.reduce(jnp.maximum, carries[:n_first]))
    worst = jnp.where(io > 0, jnp.maximum(worst, worst_first), worst)
    for qi in range(n_q):
        write_out(slice(qi * TQ, (qi + 1) * TQ), accs[qi])
        acc_scr[qi] = accs[qi]
        carry_scr[qi] = carries[qi]

    def finish(qi, _):
        q_rows = pl.ds(pl.multiple_of(qi * TQ, TQ), TQ)
        q2 = stack_heads(q_ref[0, q_rows, :])

        def unfinished():
            return jnp.max(carry_scr[qi]) > EXP2_ZERO_BOUND

        def cond(state):
            k_end, go = state
            return jnp.logical_and(k_end > 0, go)

        def body(state):
            k_end, _ = state
            k_start = pl.multiple_of(jnp.maximum(k_end - TK, 0), TQ)
            kblk = k_ref[0, pl.ds(k_start, TK), :]
            vblk = v_ref[0, pl.ds(k_start, TK), :]
            carry = carry_scr[qi]
            mask = (far_col + k_start) < k_end
            log_beta, sp = logits(q2, kblk, mask)
            neg_cum = _dot(sp, cum_far)
            acc_scr[qi] += _dot(weights(log_beta, neg_cum, carry, mask), vblk)
            carry_scr[qi] = carry + neg_cum[:, 0:1] - sp[:, 0:1].astype(F32)
            return k_end - TK, unfinished()

        lax.while_loop(cond, body, (io * TQ_OUTER + qi * TQ - TK, unfinished()))
        write_out(q_rows, acc_scr[qi])
        return 0

    @pl.when(worst > EXP2_ZERO_BOUND)
    def _():
        lax.fori_loop(0, n_q, finish, 0)


def _attention(q, k, v, cast_weights=()):
    B, S, W = q.shape
    n_pairs = W // LANES
    n_io = S // TQ_OUTER
    n_steps = B * n_pairs * n_io
    qspec = pl.BlockSpec((1, TQ_OUTER, LANES), lambda b, p, i: (b, i, p))
    kvspec = pl.BlockSpec((1, S, LANES), lambda b, p, i: (b, 0, p))
    step = lambda b, p, i: (b * n_pairs + p) * n_io + i
    w_in_specs, w_out_specs, w_out_shapes = [], [], []
    for w, layer in cast_weights:
        _, R, C = w.shape
        rows = R // n_steps
        assert rows * n_steps == R and rows % (2 * SUBLANES) == 0, (R, n_steps)
        w_in_specs.append(pl.BlockSpec(
            (None, rows, C), lambda b, p, i, layer=layer: (layer, step(b, p, i), 0)))
        w_out_specs.append(pl.BlockSpec((rows, C), lambda b, p, i: (step(b, p, i), 0)))
        w_out_shapes.append(jax.ShapeDtypeStruct((R, C), BF16))
    out = pl.pallas_call(
        functools.partial(_attention_body, len(cast_weights)),
        grid=(B, n_pairs, n_io),
        in_specs=[qspec, kvspec, kvspec] + w_in_specs,
        out_specs=[qspec] + w_out_specs,
        out_shape=[jax.ShapeDtypeStruct((B, S, W), BF16)] + w_out_shapes,
        scratch_shapes=[
            pltpu.VMEM((TQ_OUTER // TQ, 2 * TQ, LANES), F32),
            pltpu.VMEM((TQ_OUTER // TQ, 2 * TQ, 1), F32),
        ],
        compiler_params=pltpu.CompilerParams(
            dimension_semantics=("arbitrary", "arbitrary", "arbitrary"),
            vmem_limit_bytes=VMEM_LIMIT_BYTES),
        name="attention",
    )(q, k, v, *(w for w, _ in cast_weights))
    return out[0], tuple(out[1:])


def _merge_ffn_body(layer, x_ref, mab_ref, gc_ref, yc_ref, wc_ref, wo_ref, fg_ref, wgu_ref, wd_ref,
                    o_ref, act_scr):
    def stage_branch(st):
        st["ydc"] = _dot(yc_ref[st["rows"], :], wc_ref[...])

    def stage_wo(st):
        rows = st["rows"]
        merged = mab_ref[rows, :].astype(F32) + gc_ref[rows, :].astype(F32) * st["ydc"]
        x1 = x_ref[rows, :] + _dot(merged.astype(BF16), wo_ref[...])
        o_ref[rows, :] = x1
        ms = jnp.mean(x1 * x1, axis=-1, keepdims=True)
        st["h2"] = (x1 * lax.rsqrt(ms + EPS) * fg_ref[layer:layer + 1, :]).astype(BF16)

    def stage_gate_up(st, c0):
        c1 = min(c0 + FF_SPLIT, D_FF)
        gate = _dot(st["h2"], wgu_ref[:, c0:c1])
        up = _dot(st["h2"], wgu_ref[:, D_FF + c0:D_FF + c1])
        act_scr[st["rows"], c0:c1] = (gate * _sigmoid(gate) * up).astype(BF16)

    def stage_down(st):
        rows = st["rows"]
        o_ref[rows, :] += _dot(act_scr[rows, :], wd_ref[...])

    subs = [dict(rows=slice(r0, r0 + TS_FFN)) for r0 in range(0, x_ref.shape[0], TS_FFN)]
    for stage in (stage_branch, stage_wo):
        for st in subs:
            stage(st)
    for c0 in range(0, D_FF, FF_SPLIT):
        for st in subs:
            stage_gate_up(st, c0)
    for st in subs:
        stage_down(st)


def _merge_ffn(l, x, mab, gc, yc, wbo, wo, fg, wgu, wd):
    M, D = x.shape
    tm = TM_FFN
    tok = lambda width: pl.BlockSpec((tm, width), lambda i: (i, 0))
    return pl.pallas_call(
        functools.partial(_merge_ffn_body, l),
        grid=(M // tm,),
        in_specs=[
            tok(D), tok(D), tok(D), tok(ATT_WIDTH),
            pl.BlockSpec((None, CONV_CH, D), lambda *_: (2, 0, 0),
                         pipeline_mode=pl.Buffered(1)),
            _resident(wo.shape), _resident(fg.shape),
            _resident(wgu.shape), _resident(wd.shape),
        ],
        out_specs=tok(D),
        out_shape=jax.ShapeDtypeStruct((M, D), F32),
        scratch_shapes=[pltpu.VMEM((tm, D_FF), BF16)],
        compiler_params=pltpu.CompilerParams(
            dimension_semantics=("arbitrary",),
            vmem_limit_bytes=VMEM_LIMIT_BYTES),
        name="merge_ffn",
    )(x, mab, gc, yc, wbo, wo, fg, wgu, wd)


def kernel(x, mix_norm_g, w_in, b_gate, conv_w, conv_b, sgu_ln_g, sgu_ln_b, sgu_w, sgu_b,
           q_norm_g, k_norm_g, w_branch_out, w_o, ffn_norm_g, w_gate_up, w_down):
    B, S, D = x.shape
    depth = w_in.shape[0]
    assert D == D_MODEL and w_in.shape[1:] == (D_MODEL, IN_COLS), (x.shape, w_in.shape)
    assert conv_w.shape[1:] == (CONV_K, CONV_CH) and CONV_K == 3, conv_w.shape
    assert sgu_w.shape[1:] == (SGU_GROUPS, CHUNK, CHUNK), sgu_w.shape
    assert w_gate_up.shape[1:] == (D_MODEL, 2 * D_FF) and w_down.shape[1:] == (D_FF, D_MODEL)
    assert S % TQ_OUTER == 0 and S % TM_IN == 0 and (B * S) % TM_FFN == 0, (B, S)
    sbf = jnp.repeat(jnp.swapaxes(sgu_b, 1, 2), CHUNK, axis=2)
    qg = jnp.tile(q_norm_g, (1, LANES // HEAD_DIM))
    kg = jnp.tile(k_norm_g, (1, LANES // HEAD_DIM))
    w_bo = w_branch_out.reshape(depth, N_BRANCH * CONV_CH, D)
    win, wbo = _cast_bf16(w_in, 0), _cast_bf16(w_bo, 0)
    for l in range(depth):
        wbo = wbo.reshape(N_BRANCH, CONV_CH, D)
        mab, gc, q, k, v = _mixer_in(
            l, x, mix_norm_g, win, b_gate, conv_w, conv_b, sgu_ln_g, sgu_ln_b, sgu_w, sbf,
            qg, kg, wbo)
        casts = [(w_o, l), (w_gate_up, l), (w_down, l)]
        if l + 1 < depth:
            casts += [(w_in, l + 1), (w_bo, l + 1)]
        yc, cast = _attention(q, k, v, casts)
        wo, wgu, wd = cast[:3]
        x = _merge_ffn(
            l, x.reshape(B * S, D), mab.reshape(B * S, D), gc.reshape(B * S, D),
            yc.reshape(B * S, ATT_WIDTH), wbo, wo, ffn_norm_g, wgu, wd).reshape(B, S, D)
        if l + 1 < depth:
            win, wbo = cast[3:]
    return x
```

```python
import functools

import jax
import jax.numpy as jnp
from jax import lax
from jax.experimental import pallas as pl
from jax.experimental.pallas import tpu as pltpu

F32 = jnp.float32
BF16 = jnp.bfloat16

D_MODEL = 1024
CONV_CH = 512
CONV_K = 3
SGU_WIDTH = 512
SGU_GROUPS = 4
CHUNK = 128
ATT_HEADS = 8
HEAD_DIM = 64
ATT_WIDTH = ATT_HEADS * HEAD_DIM
N_BRANCH = 3
D_FF = 2816
IN_COLS = 7168
EPS = 1e-6

LANES = 128
SUBLANES = 8
VMEM_LIMIT_BYTES = 58 * 1024 * 1024

C_AB, C_AC, C_AX = 0, CONV_CH, 2 * CONV_CH
C_SU = 3 * CONV_CH
C_SV = C_SU + SGU_WIDTH
C_Q = C_SV + SGU_WIDTH
C_K, C_V = C_Q + ATT_WIDTH, C_Q + 2 * ATT_WIDTH
C_G = C_Q + 3 * ATT_WIDTH
assert C_G + N_BRANCH * D_MODEL == IN_COLS

TM_IN = 1024
TS_IN = 512
TM_FFN = 1024
TS_FFN = 512
TQ_OUTER = 4096
TQ = 128
TK = 256
MXU_DIM = 256
FF_SPLIT = 6 * MXU_DIM
CAST_BLOCK_BYTES = 8 * 1024 * 1024

LOG2E = 1.4426950408889634
EXP2_CLAMP = 126.0
STAGE_SKEW = 2

EXP2_ZERO_BOUND = -151.0


def _sigmoid(x):
    return 1.0 / (1.0 + jnp.exp(-x))


def _gelu_exact(x):
    return 0.5 * x * (1.0 + lax.erf(x * (2.0 ** -0.5)))


def _dot(a, b):
    return jnp.dot(a, b, preferred_element_type=F32)


def _resident(shape):
    zeros = (0,) * len(shape)
    return pl.BlockSpec(shape, lambda *_: zeros, pipeline_mode=pl.Buffered(1))


def _resident_layer(shape, layer):
    index = (layer,) + (0,) * (len(shape) - 1)
    return pl.BlockSpec((None,) + tuple(shape[1:]), lambda *_: index,
                        pipeline_mode=pl.Buffered(1))


def _cast_body(w_ref, o_ref):
    o_ref[...] = w_ref[...].astype(o_ref.dtype)


def _cast_rows(n_rows, n_cols):
    best = None
    for rows in range(2 * SUBLANES, n_rows + 1, 2 * SUBLANES):
        if n_rows % rows == 0 and rows * n_cols * 4 <= CAST_BLOCK_BYTES:
            best = rows
    assert best is not None, (n_rows, n_cols)
    return best


def _cast_bf16(w, layer):
    _, R, C = w.shape
    rows = _cast_rows(R, C)
    return pl.pallas_call(
        _cast_body,
        grid=(R // rows,),
        in_specs=[pl.BlockSpec((None, rows, C), lambda r: (layer, r, 0))],
        out_specs=pl.BlockSpec((rows, C), lambda r: (r, 0)),
        out_shape=jax.ShapeDtypeStruct((R, C), BF16),
        compiler_params=pltpu.CompilerParams(
            dimension_semantics=("arbitrary",),
            vmem_limit_bytes=VMEM_LIMIT_BYTES),
        name="cast_bf16",
    )(w)


def _head_rmsnorm(p, g128):
    tm = p.shape[0]
    lo = lax.broadcasted_iota(jnp.int32, (tm, LANES), 1) < HEAD_DIM
    outs = []
    for c in range(ATT_WIDTH // LANES):
        pc = p[:, c * LANES:(c + 1) * LANES]
        pp = pc * pc
        s_lo = jnp.sum(jnp.where(lo, pp, 0.0), axis=-1, keepdims=True)
        s_hi = jnp.sum(jnp.where(lo, 0.0, pp), axis=-1, keepdims=True)
        ms = jnp.where(lo, s_lo, s_hi) * (1.0 / HEAD_DIM)
        outs.append(pc * lax.rsqrt(ms + EPS) * g128)
    return jnp.concatenate(outs, axis=-1)


def _mixer_in_body(layer, x_ref, xn_ref, ng_ref, win_ref, bg_ref, cw_ref, cb_ref, lng_ref, lnb_ref,
                   sw_ref, sb_ref, qg_ref, kg_ref, wab_ref,
                   mab_ref, gc_ref, q_ref, k_ref, v_ref,
                   u_scr, yb_scr, h_scr):
    tm = x_ref.shape[1]
    ts = TS_IN
    j = pl.program_id(1)
    lrow = slice(layer, layer + 1)

    @pl.when(j == 0)
    def _():
        u_scr[0:SUBLANES, :] = jnp.zeros((SUBLANES, CONV_CH), F32)

    row = lax.broadcasted_iota(jnp.int32, (CHUNK, CHUNK), 0)
    col = lax.broadcasted_iota(jnp.int32, (CHUNK, CHUNK), 1)
    sgu_w = [jnp.where(col <= row, sw_ref[g], 0.0).astype(BF16) for g in range(SGU_GROUPS)]

    def proj(st, c0, width):
        return _dot(st["h"], win_ref[:, c0:c0 + width])

    def norm(x):
        ms = jnp.mean(x * x, axis=-1, keepdims=True)
        return (x * lax.rsqrt(ms + EPS) * ng_ref[lrow, :]).astype(BF16)

    @pl.when(jnp.logical_and(pl.program_id(0) == 0, j == 0))
    def _():
        h_scr[...] = norm(x_ref[0, 0:ts, :])

    def stage_norm(st):
        if st["rows"].start == 0:
            st["h"] = h_scr[...]
        else:
            st["h"] = norm(x_ref[0, st["rows"], :])

    def stage_conv(st):
        u = proj(st, C_AC, CONV_CH) * proj(st, C_AX, CONV_CH)
        u_scr[SUBLANES:SUBLANES + ts, :] = u
        u1 = u_scr[SUBLANES - 1:SUBLANES - 1 + ts, :]
        u2 = u_scr[SUBLANES - 2:SUBLANES - 2 + ts, :]
        y = cb_ref[lrow, :] + cw_ref[0:1, :] * u2
        y = y + cw_ref[1:2, :] * u1
        y = y + cw_ref[2:3, :] * u
        st["ya"] = (proj(st, C_AB, CONV_CH) * y).astype(BF16)
        u_scr[0:SUBLANES, :] = u_scr[ts:ts + SUBLANES, :]

    def stage_sgu_in(st):
        st["gu"] = _gelu_exact(proj(st, C_SU, SGU_WIDTH))
        gv = _gelu_exact(proj(st, C_SV, SGU_WIDTH))
        mu = jnp.mean(gv, axis=-1, keepdims=True)
        vc = gv - mu
        var = jnp.mean(vc * vc, axis=-1, keepdims=True)
        st["vn"] = (vc * lax.rsqrt(var + EPS) * lng_ref[lrow, :] + lnb_ref[lrow, :]).astype(BF16)

    def stage_qkv(st):
        rows = st["rows"]
        q = _head_rmsnorm(proj(st, C_Q, ATT_WIDTH), qg_ref[lrow, :]) * (HEAD_DIM ** -0.5 * LOG2E)
        q_ref[0, rows, :] = q.astype(BF16)
        k_ref[0, rows, :] = _head_rmsnorm(proj(st, C_K, ATT_WIDTH), kg_ref[lrow, :]).astype(BF16)
        v_ref[0, rows, :] = proj(st, C_V, ATT_WIDTH).astype(BF16)

    def stage_sgu_mix(st):
        r0 = st["rows"].start
        for g in range(SGU_GROUPS):
            cs = slice(g * CHUNK, (g + 1) * CHUNK)
            for c in range(ts // CHUNK):
                rs = slice(c * CHUNK, (c + 1) * CHUNK)
                mixed = _dot(sgu_w[g], st["vn"][rs, cs]) + sb_ref[:, cs]
                yb_scr[r0 + c * CHUNK:r0 + (c + 1) * CHUNK, cs] = (
                    st["gu"][rs, cs] * mixed).astype(BF16)

    def stage_gates(st):
        rows = st["rows"]
        ga = _sigmoid(proj(st, C_G, D_MODEL) + bg_ref[lrow, 0:D_MODEL])
        gb = _sigmoid(proj(st, C_G + D_MODEL, D_MODEL) + bg_ref[lrow, D_MODEL:2 * D_MODEL])
        gc = _sigmoid(proj(st, C_G + 2 * D_MODEL, D_MODEL) + bg_ref[lrow, 2 * D_MODEL:3 * D_MODEL])
        gc_ref[0, rows, :] = gc.astype(BF16)
        mab = ga * _dot(st["ya"], wab_ref[0]) + gb * _dot(yb_scr[rows, :], wab_ref[1])
        mab_ref[0, rows, :] = mab.astype(BF16)

    subs = [dict(rows=slice(r0, r0 + ts)) for r0 in range(0, tm, ts)]
    for st in subs:
        stage_norm(st)
    for i in range(len(subs) + 1):
        if i < len(subs):
            stage_conv(subs[i])
            stage_sgu_in(subs[i])
            stage_qkv(subs[i])
        if i == 0:
            h_scr[...] = norm(xn_ref[0])
        if i >= 1:
            stage_sgu_mix(subs[i - 1])
            stage_gates(subs[i - 1])


def _mixer_in(l, x, ng, win, bg, cw, cb, lng, lnb, sw, sbf, qg, kg, wbo):
    B, S, D = x.shape
    tm = TM_IN
    tok = lambda width: pl.BlockSpec((1, tm, width), lambda b, j: (b, j, 0))
    n_j = S // tm

    def next_head(b, j):
        f = jnp.minimum(b * n_j + j + 1, B * n_j - 1)
        return f // n_j, (f % n_j) * (tm // TS_IN), 0

    out_shape = (
        jax.ShapeDtypeStruct((B, S, D), BF16),
        jax.ShapeDtypeStruct((B, S, D), BF16),
        jax.ShapeDtypeStruct((B, S, ATT_WIDTH), BF16),
        jax.ShapeDtypeStruct((B, S, ATT_WIDTH), BF16),
        jax.ShapeDtypeStruct((B, S, ATT_WIDTH), BF16),
    )
    return pl.pallas_call(
        functools.partial(_mixer_in_body, l),
        grid=(B, S // tm),
        in_specs=[
            tok(D), pl.BlockSpec((1, TS_IN, D), next_head),
            _resident(ng.shape), _resident(win.shape), _resident(bg.shape),
            _resident_layer(cw.shape, l), _resident(cb.shape),
            _resident(lng.shape), _resident(lnb.shape),
            _resident_layer(sw.shape, l), _resident_layer(sbf.shape, l),
            _resident(qg.shape), _resident(kg.shape),
            _resident((2, CONV_CH, D)),
        ],
        out_specs=(tok(D), tok(D), tok(ATT_WIDTH), tok(ATT_WIDTH), tok(ATT_WIDTH)),
        out_shape=out_shape,
        scratch_shapes=[
            pltpu.VMEM((TS_IN + SUBLANES, CONV_CH), F32),
            pltpu.VMEM((tm, SGU_WIDTH), BF16),
            pltpu.VMEM((TS_IN, D), BF16),
        ],
        compiler_params=pltpu.CompilerParams(
            dimension_semantics=("arbitrary", "arbitrary"),
            vmem_limit_bytes=VMEM_LIMIT_BYTES),
        name="mixer_in",
    )(x, x, ng, win, bg, cw, cb, lng, lnb, sw, sbf, qg, kg, wbo)


def _attention_body(n_cast, q_ref, k_ref, v_ref, *refs):
    w_refs, o_ref = refs[:n_cast], refs[n_cast]
    wo_refs = refs[n_cast + 1:2 * n_cast + 1]
    acc_scr, carry_scr = refs[2 * n_cast + 1:]
    for w_ref, wo_ref in zip(w_refs, wo_refs):
        wo_ref[...] = w_ref[...].astype(wo_ref.dtype)

    io = pl.program_id(2)
    n_q = TQ_OUTER // TQ
    lo = lax.broadcasted_iota(jnp.int32, (TQ, LANES), 1) < HEAD_DIM

    def neg_exclusive(n):
        return jnp.where(lax.broadcasted_iota(jnp.int32, (n, n), 0)
                         > lax.broadcasted_iota(jnp.int32, (n, n), 1), -1.0, 0.0).astype(BF16)

    cum_diag = jnp.concatenate([neg_exclusive(TQ), jnp.full((TQ, LANES), -1.0, BF16)], axis=1)
    cum_far = neg_exclusive(TK)
    diag_mask = (lax.broadcasted_iota(jnp.int32, (2 * TQ, TQ), 1)
                 < lax.broadcasted_iota(jnp.int32, (2 * TQ, TQ), 0) % TQ)
    far_col = lax.broadcasted_iota(jnp.int32, (2 * TQ, TK), 1)

    def stack_heads(q):
        zero = jnp.zeros_like(q)
        return jnp.concatenate([jnp.where(lo, q, zero), jnp.where(lo, zero, q)], axis=0)

    def logits(q2, kblk, mask):
        z = lax.dot_general(q2, kblk, (((1,), (1,)), ((), ())), preferred_element_type=F32)
        sp = jnp.maximum(z, jnp.log(1.0 + jnp.exp2(jnp.minimum(z, EXP2_CLAMP))) * LOG2E)
        log_beta = z - sp
        if mask is not None:
            sp = jnp.where(mask, sp, 0.0)
        return log_beta, sp.astype(BF16)

    def weights(log_beta, neg_cum, carry, mask):
        e = log_beta + neg_cum
        if carry is not None:
            e = e + carry
        a = jnp.exp2(e)
        if mask is not None:
            a = jnp.where(mask, a, 0.0)
        return a.astype(BF16)

    def write_out(q_rows, acc):
        o_ref[0, q_rows, :] = jnp.where(lo, acc[:TQ], acc[TQ:]).astype(o_ref.dtype)

    def stage_logits(qi):
        q_start = pl.multiple_of(io * TQ_OUTER + qi * TQ, TQ)
        far_start = pl.multiple_of(jnp.maximum(q_start - TK, 0), TQ)
        q2 = stack_heads(q_ref[0, qi * TQ:(qi + 1) * TQ, :])
        far_mask = (far_col + far_start) < q_start if qi * TQ < TK else None
        lb_d, sp_d = logits(q2, k_ref[0, pl.ds(q_start, TQ), :], diag_mask)
        lb_f, sp_f = logits(q2, k_ref[0, pl.ds(far_start, TK), :], far_mask)
        return dict(q_start=q_start, far_start=far_start, far_mask=far_mask,
                    lb_d=lb_d, sp_d=sp_d, lb_f=lb_f, sp_f=sp_f)

    def stage_weights(st):
        neg_d = _dot(st["sp_d"], cum_diag)
        neg_f = _dot(st["sp_f"], cum_far)
        carry_d = neg_d[:, TQ:]
        a_d = weights(st["lb_d"], neg_d[:, :TQ], None, diag_mask)
        a_f = weights(st["lb_f"], neg_f, jnp.concatenate([carry_d] * (TK // LANES), axis=1),
                      st["far_mask"])
        return dict(q_start=st["q_start"], far_start=st["far_start"],
                    a_d=a_d, a_f=a_f,
                    carry=carry_d[:, 0:1] + neg_f[:, 0:1] - st["sp_f"][:, 0:1].astype(F32))

    def stage_values(st):
        pv_d = _dot(st["a_d"], v_ref[0, pl.ds(st["q_start"], TQ), :])
        pv_f = _dot(st["a_f"], v_ref[0, pl.ds(st["far_start"], TK), :])
        return pv_d + pv_f, st["carry"]

    after_logits, after_weights, accs, carries = {}, {}, [], []
    for t in range(n_q + 2 * STAGE_SKEW):
        if t < n_q:
            after_logits[t] = stage_logits(t)
        if 0 <= t - STAGE_SKEW < n_q:
            after_weights[t - STAGE_SKEW] = stage_weights(after_logits.pop(t - STAGE_SKEW))
        if 0 <= t - 2 * STAGE_SKEW < n_q:
            acc, carry = stage_values(after_weights.pop(t - 2 * STAGE_SKEW))
            accs.append(acc)
            carries.append(carry)

    n_first = TK // TQ + 1
    worst = jnp.max(functools.reduce(jnp.maximum, carries[n_first:]))
    worst_first = jnp.max(functools.reduce(jnp.maximum, carries[:n_first]))
    worst = jnp.where(io > 0, jnp.maximum(worst, worst_first), worst)
    for qi in range(n_q):
        write_out(slice(qi * TQ, (qi + 1) * TQ), accs[qi])
        acc_scr[qi] = accs[qi]
        carry_scr[qi] = carries[qi]

    def finish(qi, _):
        q_rows = pl.ds(pl.multiple_of(qi * TQ, TQ), TQ)
        q2 = stack_heads(q_ref[0, q_rows, :])

        def unfinished():
            return jnp.max(carry_scr[qi]) > EXP2_ZERO_BOUND

        def cond(state):
            k_end, go = state
            return jnp.logical_and(k_end > 0, go)

        def body(state):
            k_end, _ = state
            k_start = pl.multiple_of(jnp.maximum(k_end - TK, 0), TQ)
            kblk = k_ref[0, pl.ds(k_start, TK), :]
            vblk = v_ref[0, pl.ds(k_start, TK), :]
            carry = carry_scr[qi]
            mask = (far_col + k_start) < k_end
            log_beta, sp = logits(q2, kblk, mask)
            neg_cum = _dot(sp, cum_far)
            acc_scr[qi] += _dot(weights(log_beta, neg_cum, carry, mask), vblk)
            carry_scr[qi] = carry + neg_cum[:, 0:1] - sp[:, 0:1].astype(F32)
            return k_end - TK, unfinished()

        lax.while_loop(cond, body, (io * TQ_OUTER + qi * TQ - TK, unfinished()))
        write_out(q_rows, acc_scr[qi])
        return 0

    @pl.when(worst > EXP2_ZERO_BOUND)
    def _():
        lax.fori_loop(0, n_q, finish, 0)


def _attention(q, k, v, cast_weights=()):
    B, S, W = q.shape
    n_pairs = W // LANES
    n_io = S // TQ_OUTER
    n_steps = B * n_pairs * n_io
    qspec = pl.BlockSpec((1, TQ_OUTER, LANES), lambda b, p, i: (b, i, p))
    kvspec = pl.BlockSpec((1, S, LANES), lambda b, p, i: (b, 0, p))
    step = lambda b, p, i: (b * n_pairs + p) * n_io + i
    w_in_specs, w_out_specs, w_out_shapes = [], [], []
    for w, layer in cast_weights:
        _, R, C = w.shape
        rows = R // n_steps
        assert rows * n_steps == R and rows % (2 * SUBLANES) == 0, (R, n_steps)
        w_in_specs.append(pl.BlockSpec(
            (None, rows, C), lambda b, p, i, layer=layer: (layer, step(b, p, i), 0)))
        w_out_specs.append(pl.BlockSpec((rows, C), lambda b, p, i: (step(b, p, i), 0)))
        w_out_shapes.append(jax.ShapeDtypeStruct((R, C), BF16))
    out = pl.pallas_call(
        functools.partial(_attention_body, len(cast_weights)),
        grid=(B, n_pairs, n_io),
        in_specs=[qspec, kvspec, kvspec] + w_in_specs,
        out_specs=[qspec] + w_out_specs,
        out_shape=[jax.ShapeDtypeStruct((B, S, W), BF16)] + w_out_shapes,
        scratch_shapes=[
            pltpu.VMEM((TQ_OUTER // TQ, 2 * TQ, LANES), F32),
            pltpu.VMEM((TQ_OUTER // TQ, 2 * TQ, 1), F32),
        ],
        compiler_params=pltpu.CompilerParams(
            dimension_semantics=("arbitrary", "arbitrary", "arbitrary"),
            vmem_limit_bytes=VMEM_LIMIT_BYTES),
        name="attention",
    )(q, k, v, *(w for w, _ in cast_weights))
    return out[0], tuple(out[1:])


def _merge_ffn_body(layer, x_ref, mab_ref, gc_ref, yc_ref, wc_ref, wo_ref, fg_ref, wgu_ref, wd_ref,
                    o_ref, act_scr):
    def stage_branch(st):
        st["ydc"] = _dot(yc_ref[st["rows"], :], wc_ref[...])

    def stage_wo(st):
        rows = st["rows"]
        merged = mab_ref[rows, :].astype(F32) + gc_ref[rows, :].astype(F32) * st["ydc"]
        x1 = x_ref[rows, :] + _dot(merged.astype(BF16), wo_ref[...])
        o_ref[rows, :] = x1
        ms = jnp.mean(x1 * x1, axis=-1, keepdims=True)
        st["h2"] = (x1 * lax.rsqrt(ms + EPS) * fg_ref[layer:layer + 1, :]).astype(BF16)

    def stage_gate_up(st, c0):
        c1 = min(c0 + FF_SPLIT, D_FF)
        gate = _dot(st["h2"], wgu_ref[:, c0:c1])
        up = _dot(st["h2"], wgu_ref[:, D_FF + c0:D_FF + c1])
        act_scr[st["rows"], c0:c1] = (gate * _sigmoid(gate) * up).astype(BF16)

    def stage_down(st):
        rows = st["rows"]
        o_ref[rows, :] += _dot(act_scr[rows, :], wd_ref[...])

    subs = [dict(rows=slice(r0, r0 + TS_FFN)) for r0 in range(0, x_ref.shape[0], TS_FFN)]
    for stage in (stage_branch, stage_wo):
        for st in subs:
            stage(st)
    for c0 in range(0, D_FF, FF_SPLIT):
        for st in subs:
            stage_gate_up(st, c0)
    for st in subs:
        stage_down(st)


def _merge_ffn(l, x, mab, gc, yc, wbo, wo, fg, wgu, wd):
    M, D = x.shape
    tm = TM_FFN
    tok = lambda width: pl.BlockSpec((tm, width), lambda i: (i, 0))
    return pl.pallas_call(
        functools.partial(_merge_ffn_body, l),
        grid=(M // tm,),
        in_specs=[
            tok(D), tok(D), tok(D), tok(ATT_WIDTH),
            pl.BlockSpec((None, CONV_CH, D), lambda *_: (2, 0, 0),
                         pipeline_mode=pl.Buffered(1)),
            _resident(wo.shape), _resident(fg.shape),
            _resident(wgu.shape), _resident(wd.shape),
        ],
        out_specs=tok(D),
        out_shape=jax.ShapeDtypeStruct((M, D), F32),
        scratch_shapes=[pltpu.VMEM((tm, D_FF), BF16)],
        compiler_params=pltpu.CompilerParams(
            dimension_semantics=("arbitrary",),
            vmem_limit_bytes=VMEM_LIMIT_BYTES),
        name="merge_ffn",
    )(x, mab, gc, yc, wbo, wo, fg, wgu, wd)


def kernel(x, mix_norm_g, w_in, b_gate, conv_w, conv_b, sgu_ln_g, sgu_ln_b, sgu_w, sgu_b,
           q_norm_g, k_norm_g, w_branch_out, w_o, ffn_norm_g, w_gate_up, w_down):
    B, S, D = x.shape
    depth = w_in.shape[0]
    assert D == D_MODEL and w_in.shape[1:] == (D_MODEL, IN_COLS), (x.shape, w_in.shape)
    assert conv_w.shape[1:] == (CONV_K, CONV_CH) and CONV_K == 3, conv_w.shape
    assert sgu_w.shape[1:] == (SGU_GROUPS, CHUNK, CHUNK), sgu_w.shape
    assert w_gate_up.shape[1:] == (D_MODEL, 2 * D_FF) and w_down.shape[1:] == (D_FF, D_MODEL)
    assert S % TQ_OUTER == 0 and S % TM_IN == 0 and (B * S) % TM_FFN == 0, (B, S)
    sbf = jnp.repeat(jnp.swapaxes(sgu_b, 1, 2), CHUNK, axis=2)
    qg = jnp.tile(q_norm_g, (1, LANES // HEAD_DIM))
    kg = jnp.tile(k_norm_g, (1, LANES // HEAD_DIM))
    w_bo = w_branch_out.reshape(depth, N_BRANCH * CONV_CH, D)
    win, wbo = _cast_bf16(w_in, 0), _cast_bf16(w_bo, 0)
    for l in range(depth):
        wbo = wbo.reshape(N_BRANCH, CONV_CH, D)
        mab, gc, q, k, v = _mixer_in(
            l, x, mix_norm_g, win, b_gate, conv_w, conv_b, sgu_ln_g, sgu_ln_b, sgu_w, sbf,
            qg, kg, wbo)
        casts = [(w_o, l), (w_gate_up, l), (w_down, l)]
        if l + 1 < depth:
            casts += [(w_in, l + 1), (w_bo, l + 1)]
        yc, cast = _attention(q, k, v, casts)
        wo, wgu, wd = cast[:3]
        x = _merge_ffn(
            l, x.reshape(B * S, D), mab.reshape(B * S, D), gc.reshape(B * S, D),
            yc.reshape(B * S, ATT_WIDTH), wbo, wo, ffn_norm_g, wgu, wd).reshape(B, S, D)
        if l + 1 < depth:
            win, wbo = cast[3:]
    return x
```

```python
import functools

import jax
import jax.numpy as jnp
from jax import lax
from jax.experimental import pallas as pl
from jax.experimental.pallas import tpu as pltpu

F32 = jnp.float32
BF16 = jnp.bfloat16

D_MODEL = 1024
CONV_CH = 512
CONV_K = 3
SGU_WIDTH = 512
SGU_GROUPS = 4
CHUNK = 128
ATT_HEADS = 8
HEAD_DIM = 64
ATT_WIDTH = ATT_HEADS * HEAD_DIM
N_BRANCH = 3
D_FF = 2816
IN_COLS = 7168
EPS = 1e-6

LANES = 128
SUBLANES = 8
VMEM_LIMIT_BYTES = 56 * 1024 * 1024

C_AB, C_AC, C_AX = 0, CONV_CH, 2 * CONV_CH
C_SU = 3 * CONV_CH
C_SV = C_SU + SGU_WIDTH
C_Q = C_SV + SGU_WIDTH
C_K, C_V = C_Q + ATT_WIDTH, C_Q + 2 * ATT_WIDTH
C_G = C_Q + 3 * ATT_WIDTH
assert C_G + N_BRANCH * D_MODEL == IN_COLS

TM_IN = 1024
TS_IN = 512
TM_FFN = 1024
TS_FFN = 512
TQ_OUTER = 4096
TQ = 128
TK = 256
MXU_DIM = 256
FF_SPLIT = 6 * MXU_DIM
CAST_BLOCK_BYTES = 8 * 1024 * 1024

LOG2E = 1.4426950408889634
EXP2_CLAMP = 126.0
STAGE_SKEW = 2

EXP2_ZERO_BOUND = -151.0


def _sigmoid(x):
    return 1.0 / (1.0 + jnp.exp(-x))


def _gelu_exact(x):
    return 0.5 * x * (1.0 + lax.erf(x * (2.0 ** -0.5)))


def _dot(a, b):
    return jnp.dot(a, b, preferred_element_type=F32)


def _resident(shape):
    zeros = (0,) * len(shape)
    return pl.BlockSpec(shape, lambda *_: zeros, pipeline_mode=pl.Buffered(1))


def _resident_layer(shape, layer):
    index = (layer,) + (0,) * (len(shape) - 1)
    return pl.BlockSpec((None,) + tuple(shape[1:]), lambda *_: index,
                        pipeline_mode=pl.Buffered(1))


def _cast_body(w_ref, o_ref):
    o_ref[...] = w_ref[...].astype(o_ref.dtype)


def _cast_rows(n_rows, n_cols):
    best = None
    for rows in range(2 * SUBLANES, n_rows + 1, 2 * SUBLANES):
        if n_rows % rows == 0 and rows * n_cols * 4 <= CAST_BLOCK_BYTES:
            best = rows
    assert best is not None, (n_rows, n_cols)
    return best


def _cast_bf16(w, layer):
    _, R, C = w.shape
    rows = _cast_rows(R, C)
    return pl.pallas_call(
        _cast_body,
        grid=(R // rows,),
        in_specs=[pl.BlockSpec((None, rows, C), lambda r: (layer, r, 0))],
        out_specs=pl.BlockSpec((rows, C), lambda r: (r, 0)),
        out_shape=jax.ShapeDtypeStruct((R, C), BF16),
        compiler_params=pltpu.CompilerParams(
            dimension_semantics=("arbitrary",),
            vmem_limit_bytes=VMEM_LIMIT_BYTES),
        name="cast_bf16",
    )(w)


def _head_rmsnorm(p, g128):
    tm = p.shape[0]
    lo = lax.broadcasted_iota(jnp.int32, (tm, LANES), 1) < HEAD_DIM
    outs = []
    for c in range(ATT_WIDTH // LANES):
        pc = p[:, c * LANES:(c + 1) * LANES]
        pp = pc * pc
        s_lo = jnp.sum(jnp.where(lo, pp, 0.0), axis=-1, keepdims=True)
        s_hi = jnp.sum(jnp.where(lo, 0.0, pp), axis=-1, keepdims=True)
        ms = jnp.where(lo, s_lo, s_hi) * (1.0 / HEAD_DIM)
        outs.append(pc * lax.rsqrt(ms + EPS) * g128)
    return jnp.concatenate(outs, axis=-1)


def _mixer_in_body(layer, x_ref, ng_ref, win_ref, bg_ref, cw_ref, cb_ref, lng_ref, lnb_ref,
                   sw_ref, sb_ref, qg_ref, kg_ref, wab_ref,
                   mab_ref, gc_ref, q_ref, k_ref, v_ref,
                   u_scr, yb_scr):
    tm = x_ref.shape[1]
    ts = TS_IN
    j = pl.program_id(1)
    lrow = slice(layer, layer + 1)

    @pl.when(j == 0)
    def _():
        u_scr[0:SUBLANES, :] = jnp.zeros((SUBLANES, CONV_CH), F32)

    row = lax.broadcasted_iota(jnp.int32, (CHUNK, CHUNK), 0)
    col = lax.broadcasted_iota(jnp.int32, (CHUNK, CHUNK), 1)
    sgu_w = [jnp.where(col <= row, sw_ref[g], 0.0).astype(BF16) for g in range(SGU_GROUPS)]

    def proj(st, c0, width):
        return _dot(st["h"], win_ref[:, c0:c0 + width])

    def stage_norm(st):
        x = x_ref[0, st["rows"], :]
        ms = jnp.mean(x * x, axis=-1, keepdims=True)
        st["h"] = (x * lax.rsqrt(ms + EPS) * ng_ref[lrow, :]).astype(BF16)

    def stage_conv(st):
        u = proj(st, C_AC, CONV_CH) * proj(st, C_AX, CONV_CH)
        u_scr[SUBLANES:SUBLANES + ts, :] = u
        u1 = u_scr[SUBLANES - 1:SUBLANES - 1 + ts, :]
        u2 = u_scr[SUBLANES - 2:SUBLANES - 2 + ts, :]
        y = cb_ref[lrow, :] + cw_ref[0:1, :] * u2
        y = y + cw_ref[1:2, :] * u1
        y = y + cw_ref[2:3, :] * u
        st["ya"] = (proj(st, C_AB, CONV_CH) * y).astype(BF16)
        u_scr[0:SUBLANES, :] = u_scr[ts:ts + SUBLANES, :]

    def stage_sgu_in(st):
        st["gu"] = _gelu_exact(proj(st, C_SU, SGU_WIDTH))
        gv = _gelu_exact(proj(st, C_SV, SGU_WIDTH))
        mu = jnp.mean(gv, axis=-1, keepdims=True)
        vc = gv - mu
        var = jnp.mean(vc * vc, axis=-1, keepdims=True)
        st["vn"] = (vc * lax.rsqrt(var + EPS) * lng_ref[lrow, :] + lnb_ref[lrow, :]).astype(BF16)

    def stage_qkv(st):
        rows = st["rows"]
        q = _head_rmsnorm(proj(st, C_Q, ATT_WIDTH), qg_ref[lrow, :]) * (HEAD_DIM ** -0.5 * LOG2E)
        q_ref[0, rows, :] = q.astype(BF16)
        k_ref[0, rows, :] = _head_rmsnorm(proj(st, C_K, ATT_WIDTH), kg_ref[lrow, :]).astype(BF16)
        v_ref[0, rows, :] = proj(st, C_V, ATT_WIDTH).astype(BF16)

    def stage_sgu_mix(st):
        r0 = st["rows"].start
        for g in range(SGU_GROUPS):
            cs = slice(g * CHUNK, (g + 1) * CHUNK)
            for c in range(ts // CHUNK):
                rs = slice(c * CHUNK, (c + 1) * CHUNK)
                mixed = _dot(sgu_w[g], st["vn"][rs, cs]) + sb_ref[:, cs]
                yb_scr[r0 + c * CHUNK:r0 + (c + 1) * CHUNK, cs] = (
                    st["gu"][rs, cs] * mixed).astype(BF16)

    def stage_gates(st):
        rows = st["rows"]
        ga = _sigmoid(proj(st, C_G, D_MODEL) + bg_ref[lrow, 0:D_MODEL])
        gb = _sigmoid(proj(st, C_G + D_MODEL, D_MODEL) + bg_ref[lrow, D_MODEL:2 * D_MODEL])
        gc = _sigmoid(proj(st, C_G + 2 * D_MODEL, D_MODEL) + bg_ref[lrow, 2 * D_MODEL:3 * D_MODEL])
        gc_ref[0, rows, :] = gc.astype(BF16)
        mab = ga * _dot(st["ya"], wab_ref[0]) + gb * _dot(yb_scr[rows, :], wab_ref[1])
        mab_ref[0, rows, :] = mab.astype(BF16)

    subs = [dict(rows=slice(r0, r0 + ts)) for r0 in range(0, tm, ts)]
    for st in subs:
        stage_norm(st)
    for i in range(len(subs) + 1):
        if i < len(subs):
            stage_conv(subs[i])
            stage_sgu_in(subs[i])
            stage_qkv(subs[i])
        if i >= 1:
            stage_sgu_mix(subs[i - 1])
            stage_gates(subs[i - 1])


def _mixer_in(l, x, ng, win, bg, cw, cb, lng, lnb, sw, sbf, qg, kg, wbo):
    B, S, D = x.shape
    tm = TM_IN
    tok = lambda width: pl.BlockSpec((1, tm, width), lambda b, j: (b, j, 0))
    out_shape = (
        jax.ShapeDtypeStruct((B, S, D), BF16),
        jax.ShapeDtypeStruct((B, S, D), BF16),
        jax.ShapeDtypeStruct((B, S, ATT_WIDTH), BF16),
        jax.ShapeDtypeStruct((B, S, ATT_WIDTH), BF16),
        jax.ShapeDtypeStruct((B, S, ATT_WIDTH), BF16),
    )
    return pl.pallas_call(
        functools.partial(_mixer_in_body, l),
        grid=(B, S // tm),
        in_specs=[
            tok(D),
            _resident(ng.shape), _resident(win.shape), _resident(bg.shape),
            _resident_layer(cw.shape, l), _resident(cb.shape),
            _resident(lng.shape), _resident(lnb.shape),
            _resident_layer(sw.shape, l), _resident_layer(sbf.shape, l),
            _resident(qg.shape), _resident(kg.shape),
            _resident((2, CONV_CH, D)),
        ],
        out_specs=(tok(D), tok(D), tok(ATT_WIDTH), tok(ATT_WIDTH), tok(ATT_WIDTH)),
        out_shape=out_shape,
        scratch_shapes=[
            pltpu.VMEM((TS_IN + SUBLANES, CONV_CH), F32),
            pltpu.VMEM((tm, SGU_WIDTH), BF16),
        ],
        compiler_params=pltpu.CompilerParams(
            dimension_semantics=("arbitrary", "arbitrary"),
            vmem_limit_bytes=VMEM_LIMIT_BYTES),
        name="mixer_in",
    )(x, ng, win, bg, cw, cb, lng, lnb, sw, sbf, qg, kg, wbo)


def _attention_body(n_cast, q_ref, k_ref, v_ref, *refs):
    w_refs, o_ref = refs[:n_cast], refs[n_cast]
    wo_refs = refs[n_cast + 1:2 * n_cast + 1]
    acc_scr, carry_scr = refs[2 * n_cast + 1:]
    for w_ref, wo_ref in zip(w_refs, wo_refs):
        wo_ref[...] = w_ref[...].astype(wo_ref.dtype)

    io = pl.program_id(2)
    n_q = TQ_OUTER // TQ
    lo = lax.broadcasted_iota(jnp.int32, (TQ, LANES), 1) < HEAD_DIM

    def neg_exclusive(n):
        return jnp.where(lax.broadcasted_iota(jnp.int32, (n, n), 0)
                         > lax.broadcasted_iota(jnp.int32, (n, n), 1), -1.0, 0.0).astype(BF16)

    cum_diag = jnp.concatenate([neg_exclusive(TQ), jnp.full((TQ, LANES), -1.0, BF16)], axis=1)
    cum_far = neg_exclusive(TK)
    diag_mask = (lax.broadcasted_iota(jnp.int32, (2 * TQ, TQ), 1)
                 < lax.broadcasted_iota(jnp.int32, (2 * TQ, TQ), 0) % TQ)
    far_col = lax.broadcasted_iota(jnp.int32, (2 * TQ, TK), 1)

    def stack_heads(q):
        zero = jnp.zeros_like(q)
        return jnp.concatenate([jnp.where(lo, q, zero), jnp.where(lo, zero, q)], axis=0)

    def logits(q2, kblk, mask):
        z = lax.dot_general(q2, kblk, (((1,), (1,)), ((), ())), preferred_element_type=F32)
        sp = jnp.maximum(z, jnp.log(1.0 + jnp.exp2(jnp.minimum(z, EXP2_CLAMP))) * LOG2E)
        log_beta = z - sp
        if mask is not None:
            sp = jnp.where(mask, sp, 0.0)
        return log_beta, sp.astype(BF16)

    def weights(log_beta, neg_cum, carry, mask):
        e = log_beta + neg_cum
        if carry is not None:
            e = e + carry
        a = jnp.exp2(e)
        if mask is not None:
            a = jnp.where(mask, a, 0.0)
        return a.astype(BF16)

    def write_out(q_rows, acc):
        o_ref[0, q_rows, :] = jnp.where(lo, acc[:TQ], acc[TQ:]).astype(o_ref.dtype)

    def stage_logits(qi):
        q_start = pl.multiple_of(io * TQ_OUTER + qi * TQ, TQ)
        far_start = pl.multiple_of(jnp.maximum(q_start - TK, 0), TQ)
        q2 = stack_heads(q_ref[0, qi * TQ:(qi + 1) * TQ, :])
        far_mask = (far_col + far_start) < q_start if qi * TQ < TK else None
        lb_d, sp_d = logits(q2, k_ref[0, pl.ds(q_start, TQ), :], diag_mask)
        lb_f, sp_f = logits(q2, k_ref[0, pl.ds(far_start, TK), :], far_mask)
        return dict(q_start=q_start, far_start=far_start, far_mask=far_mask,
                    lb_d=lb_d, sp_d=sp_d, lb_f=lb_f, sp_f=sp_f)

    def stage_weights(st):
        neg_d = _dot(st["sp_d"], cum_diag)
        neg_f = _dot(st["sp_f"], cum_far)
        carry_d = neg_d[:, TQ:]
        a_d = weights(st["lb_d"], neg_d[:, :TQ], None, diag_mask)
        a_f = weights(st["lb_f"], neg_f, jnp.concatenate([carry_d] * (TK // LANES), axis=1),
                      st["far_mask"])
        return dict(q_start=st["q_start"], far_start=st["far_start"],
                    a_d=a_d, a_f=a_f,
                    carry=carry_d[:, 0:1] + neg_f[:, 0:1] - st["sp_f"][:, 0:1].astype(F32))

    def stage_values(st):
        pv_d = _dot(st["a_d"], v_ref[0, pl.ds(st["q_start"], TQ), :])
        pv_f = _dot(st["a_f"], v_ref[0, pl.ds(st["far_start"], TK), :])
        return pv_d + pv_f, st["carry"]

    after_logits, after_weights, accs, carries = {}, {}, [], []
    for t in range(n_q + 2 * STAGE_SKEW):
        if t < n_q:
            after_logits[t] = stage_logits(t)
        if 0 <= t - STAGE_SKEW < n_q:
            after_weights[t - STAGE_SKEW] = stage_weights(after_logits.pop(t - STAGE_SKEW))
        if 0 <= t - 2 * STAGE_SKEW < n_q:
            acc, carry = stage_values(after_weights.pop(t - 2 * STAGE_SKEW))
            accs.append(acc)
            carries.append(carry)

    n_first = TK // TQ + 1
    worst = jnp.max(functools.reduce(jnp.maximum, carries[n_first:]))
    worst_first = jnp.max(functools.reduce(jnp.maximum, carries[:n_first]))
    worst = jnp.where(io > 0, jnp.maximum(worst, worst_first), worst)
    for qi in range(n_q):
        write_out(slice(qi * TQ, (qi + 1) * TQ), accs[qi])

    row_in_block = lax.broadcasted_iota(jnp.int32, (2 * TQ, TK), 0) % TQ

    def finish(qi, _):
        q_rows = pl.ds(pl.multiple_of(qi * TQ, TQ), TQ)
        q2 = stack_heads(q_ref[0, q_rows, :])
        q_start = io * TQ_OUTER + qi * TQ
        acc_scr[qi] = jnp.zeros((2 * TQ, LANES), F32)
        carry_scr[qi] = jnp.zeros((2 * TQ, 1), F32)

        def unfinished():
            return jnp.max(carry_scr[qi]) > EXP2_ZERO_BOUND

        def cond(state):
            k_end, go = state
            return jnp.logical_and(k_end > 0, go)

        def body(state):
            k_end, _ = state
            k_start = pl.multiple_of(jnp.maximum(k_end - TK, 0), TQ)
            kblk = k_ref[0, pl.ds(k_start, TK), :]
            vblk = v_ref[0, pl.ds(k_start, TK), :]
            carry = carry_scr[qi]
            k_pos = far_col + k_start
            mask = jnp.logical_and(k_pos < k_end, k_pos < row_in_block + q_start)
            log_beta, sp = logits(q2, kblk, mask)
            neg_cum = _dot(sp, cum_far)
            acc_scr[qi] += _dot(weights(log_beta, neg_cum, carry, mask), vblk)
            carry_scr[qi] = carry + neg_cum[:, 0:1] - sp[:, 0:1].astype(F32)
            return k_end - TK, unfinished()

        lax.while_loop(cond, body, (q_start + TQ, jnp.bool_(True)))
        write_out(q_rows, acc_scr[qi])
        return 0

    @pl.when(worst > EXP2_ZERO_BOUND)
    def _():
        lax.fori_loop(0, n_q, finish, 0)


def _attention(q, k, v, cast_weights=()):
    B, S, W = q.shape
    n_pairs = W // LANES
    n_io = S // TQ_OUTER
    n_steps = B * n_pairs * n_io
    qspec = pl.BlockSpec((1, TQ_OUTER, LANES), lambda b, p, i: (b, i, p))
    kvspec = pl.BlockSpec((1, S, LANES), lambda b, p, i: (b, 0, p))
    step = lambda b, p, i: (b * n_pairs + p) * n_io + i
    w_in_specs, w_out_specs, w_out_shapes = [], [], []
    for w, layer in cast_weights:
        _, R, C = w.shape
        rows = R // n_steps
        assert rows * n_steps == R and rows % (2 * SUBLANES) == 0, (R, n_steps)
        w_in_specs.append(pl.BlockSpec(
            (None, rows, C), lambda b, p, i, layer=layer: (layer, step(b, p, i), 0)))
        w_out_specs.append(pl.BlockSpec((rows, C), lambda b, p, i: (step(b, p, i), 0)))
        w_out_shapes.append(jax.ShapeDtypeStruct((R, C), BF16))
    out = pl.pallas_call(
        functools.partial(_attention_body, len(cast_weights)),
        grid=(B, n_pairs, n_io),
        in_specs=[qspec, kvspec, kvspec] + w_in_specs,
        out_specs=[qspec] + w_out_specs,
        out_shape=[jax.ShapeDtypeStruct((B, S, W), BF16)] + w_out_shapes,
        scratch_shapes=[
            pltpu.VMEM((TQ_OUTER // TQ, 2 * TQ, LANES), F32),
            pltpu.VMEM((TQ_OUTER // TQ, 2 * TQ, 1), F32),
        ],
        compiler_params=pltpu.CompilerParams(
            dimension_semantics=("arbitrary", "arbitrary", "arbitrary"),
            vmem_limit_bytes=VMEM_LIMIT_BYTES),
        name="attention",
    )(q, k, v, *(w for w, _ in cast_weights))
    return out[0], tuple(out[1:])


def _merge_ffn_body(layer, x_ref, mab_ref, gc_ref, yc_ref, wc_ref, wo_ref, fg_ref, wgu_ref, wd_ref,
                    o_ref, act_scr):
    def stage_branch(st):
        st["ydc"] = _dot(yc_ref[st["rows"], :], wc_ref[...])

    def stage_wo(st):
        rows = st["rows"]
        merged = mab_ref[rows, :].astype(F32) + gc_ref[rows, :].astype(F32) * st["ydc"]
        x1 = x_ref[rows, :] + _dot(merged.astype(BF16), wo_ref[...])
        o_ref[rows, :] = x1
        ms = jnp.mean(x1 * x1, axis=-1, keepdims=True)
        st["h2"] = (x1 * lax.rsqrt(ms + EPS) * fg_ref[layer:layer + 1, :]).astype(BF16)

    def stage_gate_up(st, c0):
        c1 = min(c0 + FF_SPLIT, D_FF)
        gate = _dot(st["h2"], wgu_ref[:, c0:c1])
        up = _dot(st["h2"], wgu_ref[:, D_FF + c0:D_FF + c1])
        act_scr[st["rows"], c0:c1] = (gate * _sigmoid(gate) * up).astype(BF16)

    def stage_down(st):
        rows = st["rows"]
        o_ref[rows, :] += _dot(act_scr[rows, :], wd_ref[...])

    subs = [dict(rows=slice(r0, r0 + TS_FFN)) for r0 in range(0, x_ref.shape[0], TS_FFN)]
    for stage in (stage_branch, stage_wo):
        for st in subs:
            stage(st)
    for c0 in range(0, D_FF, FF_SPLIT):
        for st in subs:
            stage_gate_up(st, c0)
    for st in subs:
        stage_down(st)


def _merge_ffn(l, x, mab, gc, yc, wbo, wo, fg, wgu, wd):
    M, D = x.shape
    tm = TM_FFN
    tok = lambda width: pl.BlockSpec((tm, width), lambda i: (i, 0))
    return pl.pallas_call(
        functools.partial(_merge_ffn_body, l),
        grid=(M // tm,),
        in_specs=[
            tok(D), tok(D), tok(D), tok(ATT_WIDTH),
            pl.BlockSpec((None, CONV_CH, D), lambda *_: (2, 0, 0),
                         pipeline_mode=pl.Buffered(1)),
            _resident(wo.shape), _resident(fg.shape),
            _resident(wgu.shape), _resident(wd.shape),
        ],
        out_specs=tok(D),
        out_shape=jax.ShapeDtypeStruct((M, D), F32),
        scratch_shapes=[pltpu.VMEM((tm, D_FF), BF16)],
        compiler_params=pltpu.CompilerParams(
            dimension_semantics=("arbitrary",),
            vmem_limit_bytes=VMEM_LIMIT_BYTES),
        name="merge_ffn",
    )(x, mab, gc, yc, wbo, wo, fg, wgu, wd)


def kernel(x, mix_norm_g, w_in, b_gate, conv_w, conv_b, sgu_ln_g, sgu_ln_b, sgu_w, sgu_b,
           q_norm_g, k_norm_g, w_branch_out, w_o, ffn_norm_g, w_gate_up, w_down):
    B, S, D = x.shape
    depth = w_in.shape[0]
    assert D == D_MODEL and w_in.shape[1:] == (D_MODEL, IN_COLS), (x.shape, w_in.shape)
    assert conv_w.shape[1:] == (CONV_K, CONV_CH) and CONV_K == 3, conv_w.shape
    assert sgu_w.shape[1:] == (SGU_GROUPS, CHUNK, CHUNK), sgu_w.shape
    assert w_gate_up.shape[1:] == (D_MODEL, 2 * D_FF) and w_down.shape[1:] == (D_FF, D_MODEL)
    assert S % TQ_OUTER == 0 and S % TM_IN == 0 and (B * S) % TM_FFN == 0, (B, S)
    sbf = jnp.repeat(jnp.swapaxes(sgu_b, 1, 2), CHUNK, axis=2)
    qg = jnp.tile(q_norm_g, (1, LANES // HEAD_DIM))
    kg = jnp.tile(k_norm_g, (1, LANES // HEAD_DIM))
    w_bo = w_branch_out.reshape(depth, N_BRANCH * CONV_CH, D)
    win, wbo = _cast_bf16(w_in, 0), _cast_bf16(w_bo, 0)
    for l in range(depth):
        wbo = wbo.reshape(N_BRANCH, CONV_CH, D)
        mab, gc, q, k, v = _mixer_in(
            l, x, mix_norm_g, win, b_gate, conv_w, conv_b, sgu_ln_g, sgu_ln_b, sgu_w, sbf,
            qg, kg, wbo)
        casts = [(w_o, l), (w_gate_up, l), (w_down, l)]
        if l + 1 < depth:
            casts += [(w_in, l + 1), (w_bo, l + 1)]
        yc, cast = _attention(q, k, v, casts)
        wo, wgu, wd = cast[:3]
        x = _merge_ffn(
            l, x.reshape(B * S, D), mab.reshape(B * S, D), gc.reshape(B * S, D),
            yc.reshape(B * S, ATT_WIDTH), wbo, wo, ffn_norm_g, wgu, wd).reshape(B, S, D)
        if l + 1 < depth:
            win, wbo = cast[3:]
    return x
```
